```python
import math
import jax
import jax.numpy as jnp
from jax import lax
import numpy as np

D_MODEL = 1024
BATCH = 4
SEQ = 4096
DEPTH = 2
DEC_BATCH = 2
DEC_SEQ = 16384
PAST_LEN = 128

N_BRANCH = 4
BRANCH_WIDTH = D_MODEL // 4
SC_WIDTH = BRANCH_WIDTH
SC_KERNEL = 3
CONF_WIDTH = BRANCH_WIDTH
CONF_KERNEL = 31
HEAD_DIM_C = 64
N_Q_C = BRANCH_WIDTH // HEAD_DIM_C
N_KV_C = 2
GROUP_C = N_Q_C // N_KV_C
WINDOW = 128
WIN_BLOCK = 128
N_HEADS_D = 4
HEAD_DIM_D = 32
V_DIM_D = 2 * HEAD_DIM_D
Q_BLOCK = 128
D_IN = 3 * SC_WIDTH + 2 * CONF_WIDTH + (N_Q_C + 2 * N_KV_C) * HEAD_DIM_C + N_HEADS_D * (4 * HEAD_DIM_D + V_DIM_D)
D_FF = 2816
FFN_KERNEL = 3
N_ADA = 6
ROPE_THETA = 10000.0
EPS = 1e-6
NEG_INF = -1e30

kernel_name = "hybrid_parallel_branch_encoder"


def rms_norm(x, g):
    xf = x.astype(jnp.float32)
    y = xf * lax.rsqrt(jnp.mean(xf * xf, axis=-1, keepdims=True) + EPS)
    return (y * g.astype(jnp.float32)).astype(x.dtype)


def layer_norm(x, g, b):
    xf = x.astype(jnp.float32)
    xc = xf - jnp.mean(xf, axis=-1, keepdims=True)
    y = xc * lax.rsqrt(jnp.mean(xc * xc, axis=-1, keepdims=True) + EPS)
    return (y * g.astype(jnp.float32) + b.astype(jnp.float32)).astype(x.dtype)


def depthwise_conv(x, w):
    k, c = w.shape
    pad = (k - 1) // 2
    return lax.conv_general_dilated(x, w.astype(x.dtype)[:, None, :], window_strides=(1,),
                                    padding=[(pad, pad)], dimension_numbers=("NWC", "WIO", "NWC"),
                                    feature_group_count=c)


def rope_tables(s, dim):
    inv = 1.0 / (ROPE_THETA ** (jnp.arange(0, dim, 2, dtype=jnp.float32) / dim))
    ang = jnp.arange(s, dtype=jnp.float32)[:, None] * inv[None, :]
    return jnp.cos(ang), jnp.sin(ang)


def apply_rope(x, cos, sin):
    shp = (1, x.shape[1]) + (1,) * (x.ndim - 3) + (cos.shape[-1],)
    c, s = cos.reshape(shp), sin.reshape(shp)
    xf = x.astype(jnp.float32)
    x1, x2 = jnp.split(xf, 2, axis=-1)
    return jnp.concatenate([x1 * c - x2 * s, x1 * s + x2 * c], axis=-1).astype(x.dtype)


def windowed_gqa_sink(q, k, v, sink):
    B, S = q.shape[0], q.shape[1]
    nb = S // WIN_BLOCK
    f32 = jnp.float32
    qb = q.astype(f32).reshape(B, nb, WIN_BLOCK, N_KV_C, GROUP_C, HEAD_DIM_C)

    def band(t):
        tp = jnp.pad(t.astype(f32), ((0, 0), (WIN_BLOCK, WIN_BLOCK), (0, 0), (0, 0)))
        tp = tp.reshape(B, nb + 2, WIN_BLOCK, N_KV_C, HEAD_DIM_C)
        return jnp.concatenate([tp[:, :-2], tp[:, 1:-1], tp[:, 2:]], axis=2)

    kb, vb = band(k), band(v)
    s = jnp.einsum("bnqhgd,bnkhd->bnhgqk", qb, kb) * (HEAD_DIM_C ** -0.5)
    a = jnp.arange(WIN_BLOCK)[:, None]
    t = jnp.arange(3 * WIN_BLOCK)[None, :]
    rel_ok = jnp.abs(t - WIN_BLOCK - a) <= WINDOW
    kpos = jnp.arange(nb)[:, None] * WIN_BLOCK - WIN_BLOCK + t
    in_ok = (kpos >= 0) & (kpos < S)
    mask = rel_ok[None] & in_ok[:, None, :]
    s = jnp.where(mask[None, :, None, None], s, NEG_INF)
    sk = sink.astype(f32).reshape(1, 1, N_KV_C, GROUP_C, 1, 1)
    m = jnp.maximum(jnp.max(s, axis=-1, keepdims=True), sk)
    p = jnp.exp(s - m)
    den = jnp.sum(p, axis=-1, keepdims=True) + jnp.exp(sk - m)
    o = jnp.einsum("bnhgqk,bnkhd->bnqhgd", p / den, vb)
    return o.reshape(B, S, N_Q_C * HEAD_DIM_C)


def differential_attention(q, k, v, lam, lam_init, g_subln):
    B, S = q.shape[0], q.shape[1]
    nb = S // Q_BLOCK
    f32 = jnp.float32
    kf, vf = k.astype(f32), v.astype(f32)
    qb = jnp.moveaxis(q.astype(f32).reshape(B, nb, Q_BLOCK, N_HEADS_D, 2, HEAD_DIM_D), 1, 0)

    def block(qblk):
        s = jnp.einsum("bqhcd,bkhcd->bhcqk", qblk, kf) * (HEAD_DIM_D ** -0.5)
        a = jax.nn.softmax(s, axis=-1)
        w = a[:, :, 0] - lam * a[:, :, 1]
        return jnp.einsum("bhqk,bkhe->bqhe", w, vf)

    o = lax.map(block, qb)
    o = jnp.moveaxis(o, 0, 1).reshape(B, S, N_HEADS_D, V_DIM_D)
    o = rms_norm(o, g_subln) * (1.0 - lam_init)
    return o.reshape(B, S, N_HEADS_D * V_DIM_D)


def token_mixers(h, layer, w_in, conv_a, conv_b, conv_b_bias, ln_b_gain, ln_b_bias, g_q_c, g_k_c, sink_c,
                 g_q_d, g_k_d, lam_d, g_subln_d, w_merge, w_branch, w_out):
    B, S, _ = h.shape
    dt = h.dtype
    sizes = [SC_WIDTH] * 3 + [CONF_WIDTH] * 2 + [N_Q_C * HEAD_DIM_C, N_KV_C * HEAD_DIM_C, N_KV_C * HEAD_DIM_C] \
        + [N_HEADS_D * 2 * HEAD_DIM_D] * 2 + [N_HEADS_D * V_DIM_D]
    z = h @ w_in
    a_b, a_c, a_v, b_val, b_gate, c_q, c_k, c_v, d_q, d_k, d_v = jnp.split(z, np.cumsum(sizes)[:-1].tolist(), axis=-1)

    y_a = a_b * depthwise_conv(a_c * a_v, conv_a)

    t = depthwise_conv(b_val * jax.nn.sigmoid(b_gate), conv_b) + conv_b_bias
    y_b = jax.nn.silu(layer_norm(t, ln_b_gain, ln_b_bias))

    cos_c, sin_c = rope_tables(S, HEAD_DIM_C)
    qc = apply_rope(rms_norm(c_q.reshape(B, S, N_Q_C, HEAD_DIM_C), g_q_c), cos_c, sin_c)
    kc = apply_rope(rms_norm(c_k.reshape(B, S, N_KV_C, HEAD_DIM_C), g_k_c), cos_c, sin_c)
    vc = c_v.reshape(B, S, N_KV_C, HEAD_DIM_C)
    y_c = windowed_gqa_sink(qc, kc, vc, sink_c).astype(dt)

    lam_init = 0.8 - 0.6 * math.exp(-0.3 * layer)
    lf = lam_d.astype(jnp.float32)
    lam = jnp.exp(jnp.sum(lf[0] * lf[1])) - jnp.exp(jnp.sum(lf[2] * lf[3])) + lam_init
    cos_d, sin_d = rope_tables(S, HEAD_DIM_D)
    qd = apply_rope(rms_norm(d_q.reshape(B, S, N_HEADS_D, 2, HEAD_DIM_D), g_q_d), cos_d, sin_d)
    kd = apply_rope(rms_norm(d_k.reshape(B, S, N_HEADS_D, 2, HEAD_DIM_D), g_k_d), cos_d, sin_d)
    vd = d_v.reshape(B, S, N_HEADS_D, V_DIM_D)
    y_d = differential_attention(qd, kd, vd, lam, lam_init, g_subln_d).astype(dt)

    branches = (y_a, y_b, y_c, y_d)
    merged = jax.nn.sigmoid(h @ w_merge[0]) * (branches[0] @ w_branch[0])
    for i in range(1, N_BRANCH):
        merged = merged + jax.nn.sigmoid(h @ w_merge[i]) * (branches[i] @ w_branch[i])
    return merged @ w_out


def conv_gated_ffn(h, w_up, conv_ffn, conv_ffn_bias, w_down):
    u = depthwise_conv(h @ w_up, conv_ffn) + conv_ffn_bias
    g, val = jnp.split(u, 2, axis=-1)
    return (jax.nn.silu(g) * val) @ w_down


def run_trunk(x, c, w_ada, b_ada, g_norm_mix, w_in, conv_a, conv_b, conv_b_bias, ln_b_gain, ln_b_bias,
              g_q_c, g_k_c, sink_c, g_q_d, g_k_d, lam_d, g_subln_d, w_merge, w_branch, w_out,
              g_norm_ffn, w_up, conv_ffn, conv_ffn_bias, w_down):
    for l in range(DEPTH):
        mod = jax.nn.silu(c) @ w_ada[l] + b_ada[l]
        sh1, sc1, g1, sh2, sc2, g2 = jnp.split(mod[:, None, :], N_ADA, axis=-1)
        h = rms_norm(x, g_norm_mix[l]) * (1.0 + sc1) + sh1
        x = x + g1 * token_mixers(h, l, w_in[l], conv_a[l], conv_b[l], conv_b_bias[l], ln_b_gain[l], ln_b_bias[l],
                                  g_q_c[l], g_k_c[l], sink_c[l], g_q_d[l], g_k_d[l], lam_d[l], g_subln_d[l],
                                  w_merge[l], w_branch[l], w_out[l])
        h = rms_norm(x, g_norm_ffn[l]) * (1.0 + sc2) + sh2
        x = x + g2 * conv_gated_ffn(h, w_up[l], conv_ffn[l], conv_ffn_bias[l], w_down[l])
    return x


def setup_inputs(seed: int = 0) -> dict:
    key = jax.random.key(seed)
    ks = iter(jax.random.split(key, 40))

    def nrm(shape, scale):
        return scale * jax.random.normal(next(ks), shape, jnp.float32)

    def gain(shape):
        return 1.0 + nrm(shape, 0.02)

    L, D = DEPTH, D_MODEL
    return {
        "x_prompt": nrm((BATCH, SEQ, D), 1.0),
        "x_sample": nrm((DEC_BATCH, DEC_SEQ, D), 1.0),
        "c_prompt": nrm((BATCH, D), 1.0),
        "c_sample": nrm((DEC_BATCH, D), 1.0),
        "w_ada": nrm((L, D, N_ADA * D), D ** -0.5),
        "b_ada": nrm((L, N_ADA * D), 0.02),
        "g_norm_mix": gain((L, D)),
        "w_in": nrm((L, D, D_IN), D ** -0.5),
        "conv_a": nrm((L, SC_KERNEL, SC_WIDTH), SC_KERNEL ** -0.5),
        "conv_b": nrm((L, CONF_KERNEL, CONF_WIDTH), CONF_KERNEL ** -0.5),
        "conv_b_bias": nrm((L, CONF_WIDTH), 0.02),
        "ln_b_gain": gain((L, CONF_WIDTH)),
        "ln_b_bias": nrm((L, CONF_WIDTH), 0.02),
        "g_q_c": gain((L, HEAD_DIM_C)),
        "g_k_c": gain((L, HEAD_DIM_C)),
        "sink_c": nrm((L, N_Q_C), 0.5),
        "g_q_d": gain((L, HEAD_DIM_D)),
        "g_k_d": gain((L, HEAD_DIM_D)),
        "lam_d": nrm((L, 4, HEAD_DIM_D), 0.1),
        "g_subln_d": gain((L, V_DIM_D)),
        "w_merge": nrm((L, N_BRANCH, D, D), D ** -0.5),
        "w_branch": nrm((L, N_BRANCH, BRANCH_WIDTH, D), BRANCH_WIDTH ** -0.5),
        "w_out": nrm((L, D, D), D ** -0.5),
        "g_norm_ffn": gain((L, D)),
        "w_up": nrm((L, D, 2 * D_FF), D ** -0.5),
        "conv_ffn": nrm((L, FFN_KERNEL, 2 * D_FF), FFN_KERNEL ** -0.5),
        "conv_ffn_bias": nrm((L, 2 * D_FF), 0.02),
        "w_down": nrm((L, D_FF, D), D_FF ** -0.5),
    }


def reference(x_prompt, x_sample, c_prompt, c_sample, w_ada, b_ada, g_norm_mix, w_in, conv_a, conv_b, conv_b_bias,
              ln_b_gain, ln_b_bias, g_q_c, g_k_c, sink_c, g_q_d, g_k_d, lam_d, g_subln_d, w_merge, w_branch, w_out,
              g_norm_ffn, w_up, conv_ffn, conv_ffn_bias, w_down):
    y_prompt = run_trunk(x_prompt, c_prompt, w_ada, b_ada, g_norm_mix, w_in, conv_a, conv_b, conv_b_bias,
                         ln_b_gain, ln_b_bias, g_q_c, g_k_c, sink_c, g_q_d, g_k_d, lam_d, g_subln_d,
                         w_merge, w_branch, w_out, g_norm_ffn, w_up, conv_ffn, conv_ffn_bias, w_down)
    y_sample = run_trunk(x_sample, c_sample, w_ada, b_ada, g_norm_mix, w_in, conv_a, conv_b, conv_b_bias,
                         ln_b_gain, ln_b_bias, g_q_c, g_k_c, sink_c, g_q_d, g_k_d, lam_d, g_subln_d,
                         w_merge, w_branch, w_out, g_norm_ffn, w_up, conv_ffn, conv_ffn_bias, w_down)
    return (y_prompt, y_sample)
```

```python
import functools
import math

import jax
import jax.numpy as jnp
from jax import lax
from jax.experimental import pallas as pl
from jax.experimental.pallas import tpu as pltpu

F32 = jnp.float32
BF16 = jnp.bfloat16

D_MODEL = 1024
DEPTH = 2
BRANCH = 256
HEAD_C = 64
WINDOW = 128
HEAD_D = 32
V_D = 64
N_HEADS_D = 4
CONF_K = 31
D_FF = 2816
D_IN = 2560
N_ADA = 6
ROPE_THETA = 10000.0
EPS = 1e-6
NEG = -1e30
LOG2E = 1.4426950408889634

VMEM_LIMIT = 56 * 1024 * 1024

TM = 512
HALO = 16
TQ_C = 512
TQ_D = 256
TK_D = 256
FF_CHUNKS = (512, 512, 512, 512, 512, 256)


def _cparams(sem):
    return pltpu.CompilerParams(dimension_semantics=sem, vmem_limit_bytes=VMEM_LIMIT)


def _const_spec(shape):
    nd = len(shape)
    return pl.BlockSpec(shape, lambda *_: (0,) * nd, pipeline_mode=pl.Buffered(1))


def _modulated_norm(x, gain, shift, scale):
    ms = jnp.mean(x * x, axis=-1, keepdims=True)
    return x * lax.rsqrt(ms + EPS) * gain * (1.0 + scale) + shift


def _ada_kernel(c_ref, w_ref, b_ref, o_ref):
    c = c_ref[...]
    a = c * jax.nn.sigmoid(c)
    a_hi = a.astype(BF16)
    a_lo = (a - a_hi.astype(F32)).astype(BF16)
    w = w_ref[...]
    w_hi = w.astype(BF16)
    w_lo = (w - w_hi.astype(F32)).astype(BF16)
    acc = jnp.dot(a_hi, w_hi, preferred_element_type=F32)
    acc += jnp.dot(a_lo, w_hi, preferred_element_type=F32)
    acc += jnp.dot(a_hi, w_lo, preferred_element_type=F32)
    o_ref[...] = acc + b_ref[...]


def _ada(c8, w_ada, b_ada):
    tn = 1536
    n = N_ADA * D_MODEL
    return pl.pallas_call(
        _ada_kernel,
        grid=(DEPTH, n // tn),
        in_specs=[
            pl.BlockSpec((8, D_MODEL), lambda l, j: (0, 0)),
            pl.BlockSpec((None, D_MODEL, tn), lambda l, j: (l, 0, j)),
            pl.BlockSpec((None, 1, tn), lambda l, j: (l, 0, j)),
        ],
        out_specs=pl.BlockSpec((None, 8, tn), lambda l, j: (l, 0, j)),
        out_shape=jax.ShapeDtypeStruct((DEPTH, 8, n), F32),
        compiler_params=_cparams(("arbitrary", "arbitrary")),
        name="ada",
    )(c8, w_ada, b_ada.reshape(DEPTH, 1, n))


_A0, _B0, _CQ, _CK, _CV, _DQ, _DK, _DV = 0, 768, 1280, 1536, 1664, 1792, 2048, 2304
_CQS, _CKS, _DQS, _DKS, _WEXT = 2560, 2816, 2944, 3200, 3456


def _normed_rope(z, zsw, g, gsw, cos, sin, seg, inv_n):
    sq = z * z
    hi = sq.astype(BF16)
    lo = (sq - hi.astype(F32)).astype(BF16)
    ssq = jnp.dot(hi, seg, preferred_element_type=F32) + jnp.dot(lo, seg, preferred_element_type=F32)
    r = lax.rsqrt(ssq * inv_n + EPS)
    return r * (z * (g * cos) + zsw * (gsw * sin))


def _proj_kernel(x_ref, mod_ref, gn_ref, w_ref, rope_ref, gain_ref, seg_ref,
                 za_ref, zb_ref, qc_ref, kc_ref, vc_ref, qd_ref, kd_ref, vd_ref):
    mod = mod_ref[...]
    h = _modulated_norm(x_ref[...], gn_ref[...], mod[0:1], mod[1:2])
    hb = h.astype(BF16)

    def mm(c0, c1):
        return jnp.dot(hb, w_ref[:, c0:c1], preferred_element_type=F32)

    za_ref[...] = mm(_A0, _B0).astype(BF16)
    zb_ref[...] = mm(_B0, _CQ).astype(BF16)
    vc_ref[...] = mm(_CV, _DQ).astype(BF16)
    vd_ref[...] = mm(_DV, _CQS).astype(BF16)

    rope = rope_ref[...]
    cos_c, sin_c, cos_d, sin_d = (rope[:, 0:128], rope[:, 128:256], rope[:, 256:384], rope[:, 384:512])
    two = lambda t: jnp.concatenate([t, t], axis=1)
    gains = gain_ref[...]
    seg64 = seg_ref[0]
    seg32 = seg_ref[1]
    qc_ref[...] = _normed_rope(mm(_CQ, _CK), mm(_CQS, _CKS), gains[0:1], gains[1:2],
                               two(cos_c), two(sin_c), seg64, 1.0 / HEAD_C).astype(BF16)
    kc_ref[...] = _normed_rope(mm(_CK, _CV), mm(_CKS, _DQS), gains[2:3, 0:128], gains[3:4, 0:128],
                               cos_c, sin_c, seg64[0:128, 0:128], 1.0 / HEAD_C).astype(BF16)
    qd_ref[...] = _normed_rope(mm(_DQ, _DK), mm(_DQS, _DKS), gains[4:5], gains[5:6],
                               two(cos_d), two(sin_d), seg32, 1.0 / HEAD_D).astype(BF16)
    kd_ref[...] = _normed_rope(mm(_DK, _DV), mm(_DKS, _WEXT), gains[6:7], gains[7:8],
                               two(cos_d), two(sin_d), seg32, 1.0 / HEAD_D).astype(BF16)


def _proj(x2, mod, gn, w_ext, rope, gains, seg, seq):
    t = x2.shape[0]
    tiles_per_seq = seq // TM
    row = lambda i: (i, 0)
    widths = (768, 512, 256, 128, 128, 256, 256, 256)
    return pl.pallas_call(
        _proj_kernel,
        grid=(t // TM,),
        in_specs=[
            pl.BlockSpec((TM, D_MODEL), row),
            pl.BlockSpec((None, 8, D_MODEL), lambda i: (i // tiles_per_seq, 0, 0)),
            _const_spec((1, D_MODEL)),
            _const_spec((D_MODEL, _WEXT)),
            pl.BlockSpec((TM, 512), lambda i: (i % tiles_per_seq, 0)),
            _const_spec((8, 256)),
            _const_spec((2, 256, 256)),
        ],
        out_specs=[pl.BlockSpec((TM, w), row) for w in widths],
        out_shape=[jax.ShapeDtypeStruct((t, w), BF16) for w in widths],
        compiler_params=_cparams(("parallel",)),
        name="proj",
    )(x2, mod, gn, w_ext, rope, gains, seg)


def _mix_ab_kernel(zap_ref, za_ref, zan_ref, zbp_ref, zb_ref, zbn_ref, ca_ref, cb_ref, bvec_ref,
                   ya_ref, yb_ref, abuf, bbuf, *, tiles_per_seq):
    i = pl.program_id(0)
    pos = i % tiles_per_seq
    keep_prev = (pos != 0).astype(F32)
    keep_next = (pos != tiles_per_seq - 1).astype(F32)

    def a_in(z):
        z = z.astype(F32)
        return z[:, 256:512] * z[:, 512:768]

    def b_in(z):
        z = z.astype(F32)
        return z[:, 0:256] * jax.nn.sigmoid(z[:, 256:512])

    abuf[0:HALO, :] = a_in(zap_ref[...]) * keep_prev
    abuf[HALO:HALO + TM, :] = a_in(za_ref[...])
    abuf[HALO + TM:, :] = a_in(zan_ref[...]) * keep_next
    bbuf[0:HALO, :] = b_in(zbp_ref[...]) * keep_prev
    bbuf[HALO:HALO + TM, :] = b_in(zb_ref[...])
    bbuf[HALO + TM:, :] = b_in(zbn_ref[...]) * keep_next

    ca = ca_ref[...]
    conv_a = (ca[0:1] * abuf[HALO - 1:HALO - 1 + TM, :] + ca[1:2] * abuf[HALO:HALO + TM, :]
              + ca[2:3] * abuf[HALO + 1:HALO + 1 + TM, :])
    ya_ref[...] = (za_ref[:, 0:256].astype(F32) * conv_a).astype(BF16)

    cb = cb_ref[...]
    pad = (CONF_K - 1) // 2
    t = None
    for k in range(CONF_K):
        term = cb[k:k + 1] * bbuf[HALO - pad + k:HALO - pad + k + TM, :]
        t = term if t is None else t + term
    bvec = bvec_ref[...]
    t = t + bvec[0:1]
    mu = jnp.mean(t, axis=-1, keepdims=True)
    tc = t - mu
    var = jnp.mean(tc * tc, axis=-1, keepdims=True)
    y = tc * lax.rsqrt(var + EPS) * bvec[1:2] + bvec[2:3]
    yb_ref[...] = (y * jax.nn.sigmoid(y)).astype(BF16)


def _mix_ab(za, zb, conv_a, conv_b, bvec, seq):
    t = za.shape[0]
    tiles_per_seq = seq // TM
    r = TM // HALO
    nh = t // HALO
    row = lambda i: (i, 0)
    prev = lambda i: (jnp.maximum(i * r - 1, 0), 0)
    nxt = lambda i: (jnp.minimum((i + 1) * r, nh - 1), 0)
    return pl.pallas_call(
        functools.partial(_mix_ab_kernel, tiles_per_seq=tiles_per_seq),
        grid=(t // TM,),
        in_specs=[
            pl.BlockSpec((HALO, 768), prev), pl.BlockSpec((TM, 768), row), pl.BlockSpec((HALO, 768), nxt),
            pl.BlockSpec((HALO, 512), prev), pl.BlockSpec((TM, 512), row), pl.BlockSpec((HALO, 512), nxt),
            _const_spec((8, 256)), _const_spec((32, 256)), _const_spec((8, 256)),
        ],
        out_specs=[pl.BlockSpec((TM, BRANCH), row), pl.BlockSpec((TM, BRANCH), row)],
        out_shape=[jax.ShapeDtypeStruct((t, BRANCH), BF16)] * 2,
        scratch_shapes=[pltpu.VMEM((TM + 2 * HALO, BRANCH), F32), pltpu.VMEM((TM + 2 * HALO, BRANCH), F32)],
        compiler_params=_cparams(("parallel",)),
        name="mix_ab",
    )(za, za, za, zb, zb, zb, conv_a, conv_b, bvec)


def _attn_c_kernel(q_ref, kp_ref, k_ref, kn_ref, vp_ref, v_ref, vn_ref, sink_ref, o_ref, *, tiles_per_seq):
    i = pl.program_id(0)
    pos = i % tiles_per_seq
    first = pos == 0
    last = pos == tiles_per_seq - 1
    nsub = TQ_C // WINDOW

    def halves(prev, cur, nxt):
        x = jnp.concatenate([prev[...], cur[...], nxt[...]], axis=0).astype(F32)
        xr = pltpu.roll(x, 64, axis=1)
        lane = lax.broadcasted_iota(jnp.int32, x.shape, 1)
        lo = lane < 64
        zero = jnp.zeros_like(x)
        return [[jnp.where(lo, x, zero).astype(BF16), jnp.where(lo, zero, xr).astype(BF16)],
                [jnp.where(lo, xr, zero).astype(BF16), jnp.where(lo, zero, x).astype(BF16)]]

    kz = halves(kp_ref, k_ref, kn_ref)
    vz = halves(vp_ref, v_ref, vn_ref)
    sink = sink_ref[...]
    a = lax.broadcasted_iota(jnp.int32, (WINDOW, 3 * WINDOW), 0)
    tt = lax.broadcasted_iota(jnp.int32, (WINDOW, 3 * WINDOW), 1)
    rel_ok = jnp.abs(tt - WINDOW - a) <= WINDOW

    for j in range(nsub):
        ok = rel_ok
        if j == 0:
            ok = ok & (tt >= jnp.where(first, WINDOW, 0))
        if j == nsub - 1:
            ok = ok & (tt < jnp.where(last, 2 * WINDOW, 3 * WINDOW))
        r0 = j * WINDOW
        for hk in range(2):
            qp = q_ref[r0:r0 + WINDOW, hk * 128:(hk + 1) * 128]
            out = None
            for gi in range(2):
                kk = kz[hk][gi][r0:r0 + 3 * WINDOW, :]
                vv = vz[hk][gi][r0:r0 + 3 * WINDOW, :]
                s = lax.dot_general(qp, kk, (((1,), (1,)), ((), ())), preferred_element_type=F32)
                s = jnp.where(ok, s, NEG)
                sk = sink[2 * hk + gi:2 * hk + gi + 1, 0:1]
                m = jnp.maximum(jnp.max(s, axis=-1, keepdims=True), sk)
                p = jnp.exp2(s - m)
                den = jnp.sum(p, axis=-1, keepdims=True) + jnp.exp2(sk - m)
                pv = jnp.dot((p / den).astype(BF16), vv, preferred_element_type=F32)
                out = pv if out is None else out + pv
            o_ref[r0:r0 + WINDOW, hk * 128:(hk + 1) * 128] = out.astype(BF16)


def _attn_c(qc, kc, vc, sink, seq):
    t = qc.shape[0]
    tiles_per_seq = seq // TQ_C
    r = TQ_C // WINDOW
    nh = t // WINDOW
    row = lambda i: (i, 0)
    prev = lambda i: (jnp.maximum(i * r - 1, 0), 0)
    nxt = lambda i: (jnp.minimum((i + 1) * r, nh - 1), 0)
    kv_specs = [pl.BlockSpec((WINDOW, 128), prev), pl.BlockSpec((TQ_C, 128), row), pl.BlockSpec((WINDOW, 128), nxt)]
    return pl.pallas_call(
        functools.partial(_attn_c_kernel, tiles_per_seq=tiles_per_seq),
        grid=(t // TQ_C,),
        in_specs=[pl.BlockSpec((TQ_C, BRANCH), row)] + kv_specs + kv_specs + [_const_spec((8, 128))],
        out_specs=pl.BlockSpec((TQ_C, BRANCH), row),
        out_shape=jax.ShapeDtypeStruct((t, BRANCH), BF16),
        compiler_params=_cparams(("parallel",)),
        name="attn_c",
    )(qc, kc, kc, kc, vc, vc, vc, sink)


def _attn_d_kernel(lam_ref, gs_ref, qt_ref, k_ref, vt_ref, o_ref, m_ref, l_ref, acc_ref, *, nkt, lam_init):
    qt = qt_ref[...].astype(F32)
    grp = lax.broadcasted_iota(jnp.int32, (128, TQ_D), 0) // HEAD_D
    qz = []
    for g in range(8):
        half = qt[(g // 4) * 128:(g // 4 + 1) * 128, :]
        qz.append(jnp.where(grp == (g % 4), half, 0.0).astype(BF16))
    m_ref[...] = jnp.full(m_ref.shape, NEG, F32)
    l_ref[...] = jnp.zeros(l_ref.shape, F32)
    acc_ref[...] = jnp.zeros(acc_ref.shape, F32)

    def body(kt, carry):
        kk = k_ref[kt]
        vt = vt_ref[kt]
        for g in range(8):
            h = g // 2
            s = jnp.dot(kk[:, (g // 4) * 128:(g // 4 + 1) * 128], qz[g], preferred_element_type=F32)
            m_old = m_ref[g:g + 1, :]
            m_new = jnp.maximum(m_old, jnp.max(s, axis=0, keepdims=True))
            alpha = jnp.exp2(m_old - m_new)
            p = jnp.exp2(s - m_new)
            l_ref[g:g + 1, :] = l_ref[g:g + 1, :] * alpha + jnp.sum(p, axis=0, keepdims=True)
            pv = jnp.dot(vt[h * V_D:(h + 1) * V_D, :], p.astype(BF16), preferred_element_type=F32)
            acc_ref[g] = acc_ref[g] * alpha + pv
            m_ref[g:g + 1, :] = m_new
        return carry

    lax.fori_loop(0, nkt, body, 0)

    lf = lam_ref[...]
    lam = (jnp.exp(jnp.sum(lf[0:1] * lf[1:2], axis=-1, keepdims=True))
           - jnp.exp(jnp.sum(lf[2:3] * lf[3:4], axis=-1, keepdims=True)) + lam_init)
    gs = gs_ref[...]
    for h in range(N_HEADS_D):
        o1 = acc_ref[2 * h] / l_ref[2 * h:2 * h + 1, :]
        o2 = acc_ref[2 * h + 1] / l_ref[2 * h + 1:2 * h + 2, :]
        o = o1 - lam * o2
        ms = jnp.mean(o * o, axis=0, keepdims=True)
        o_ref[h * V_D:(h + 1) * V_D, :] = o * lax.rsqrt(ms + EPS) * gs * (1.0 - lam_init)


def _attn_d(qt, kd, vt, lam_d, gs, lam_init):
    b, _, s = qt.shape
    nkt = s // TK_D
    return pl.pallas_call(
        functools.partial(_attn_d_kernel, nkt=nkt, lam_init=lam_init),
        grid=(b, s // TQ_D),
        in_specs=[
            _const_spec((4, HEAD_D)),
            _const_spec((V_D, TQ_D)),
            pl.BlockSpec((None, 256, TQ_D), lambda bi, qi: (bi, 0, qi)),
            pl.BlockSpec((None, nkt, TK_D, 256), lambda bi, qi: (bi, 0, 0, 0), pipeline_mode=pl.Buffered(1)),
            pl.BlockSpec((None, nkt, 256, TK_D), lambda bi, qi: (bi, 0, 0, 0), pipeline_mode=pl.Buffered(1)),
        ],
        out_specs=pl.BlockSpec((None, 256, TQ_D), lambda bi, qi: (bi, 0, qi)),
        out_shape=jax.ShapeDtypeStruct((b, 256, s), F32),
        scratch_shapes=[pltpu.VMEM((8, TQ_D), F32), pltpu.VMEM((8, TQ_D), F32), pltpu.VMEM((8, V_D, TQ_D), F32)],
        compiler_params=_cparams(("parallel", "arbitrary")),
        name="attn_d",
    )(lam_d, gs, qt, kd, vt)


def _merge_kernel(x_ref, mod_ref, gn_ref, ya_ref, yb_ref, yc_ref, yd_ref, wm_ref, wb_ref, wo_ref, o_ref):
    x = x_ref[...]
    mod = mod_ref[...]
    hb = _modulated_norm(x, gn_ref[...], mod[0:1], mod[1:2]).astype(BF16)
    merged = None
    for i, y_ref in enumerate((ya_ref, yb_ref, yc_ref, yd_ref)):
        gate = jax.nn.sigmoid(jnp.dot(hb, wm_ref[i], preferred_element_type=F32))
        term = gate * jnp.dot(y_ref[...], wb_ref[i], preferred_element_type=F32)
        merged = term if merged is None else merged + term
    out = jnp.dot(merged.astype(BF16), wo_ref[...], preferred_element_type=F32)
    o_ref[...] = x + mod[2:3] * out


def _merge(x2, mod, gn, ya, yb, yc, yd, wm, wb, wo, seq):
    t = x2.shape[0]
    tiles_per_seq = seq // TM
    row = lambda i: (i, 0)
    return pl.pallas_call(
        _merge_kernel,
        grid=(t // TM,),
        in_specs=[
            pl.BlockSpec((TM, D_MODEL), row),
            pl.BlockSpec((None, 8, D_MODEL), lambda i: (i // tiles_per_seq, 0, 0)),
            _const_spec((1, D_MODEL)),
        ] + [pl.BlockSpec((TM, BRANCH), row)] * 4 + [
            _const_spec((4, D_MODEL, D_MODEL)),
            _const_spec((4, BRANCH, D_MODEL)),
            _const_spec((D_MODEL, D_MODEL)),
        ],
        out_specs=pl.BlockSpec((TM, D_MODEL), row),
        out_shape=jax.ShapeDtypeStruct((t, D_MODEL), F32),
        compiler_params=_cparams(("parallel",)),
        name="merge",
    )(x2, mod, gn, ya, yb, yc, yd, wm, wb, wo)


def _ffn_kernel(xp_ref, x_ref, xn_ref, mod_ref, gn_ref, wup_ref, cw_ref, wdn_ref, o_ref, hbuf, ug, uv,
                *, tiles_per_seq):
    i = pl.program_id(0)
    pos = i % tiles_per_seq
    keep_prev = (pos != 0).astype(F32)
    keep_next = (pos != tiles_per_seq - 1).astype(F32)
    mod = mod_ref[...]
    gn = gn_ref[...]
    x = x_ref[...]
    norm = lambda v: _modulated_norm(v, gn, mod[3:4], mod[4:5])
    hbuf[0:HALO, :] = (norm(xp_ref[...]) * keep_prev).astype(BF16)
    hbuf[HALO:HALO + TM, :] = norm(x).astype(BF16)
    hbuf[HALO + TM:, :] = (norm(xn_ref[...]) * keep_next).astype(BF16)
    hb = hbuf[...]

    def conv(buf, c0, width, col):
        w = cw_ref[:, col:col + width]
        return (w[0:1] * buf[HALO - 1:HALO - 1 + TM, 0:width] + w[1:2] * buf[HALO:HALO + TM, 0:width]
                + w[2:3] * buf[HALO + 1:HALO + 1 + TM, 0:width] + w[3:4])

    acc = None
    c0 = 0
    for width in FF_CHUNKS:
        ug[:, 0:width] = jnp.dot(hb, wup_ref[:, c0:c0 + width], preferred_element_type=F32)
        uv[:, 0:width] = jnp.dot(hb, wup_ref[:, D_FF + c0:D_FF + c0 + width], preferred_element_type=F32)
        g = conv(ug, c0, width, c0)
        v = conv(uv, c0, width, D_FF + c0)
        act = (g * jax.nn.sigmoid(g) * v).astype(BF16)
        part = jnp.dot(act, wdn_ref[c0:c0 + width, :], preferred_element_type=F32)
        acc = part if acc is None else acc + part
        c0 += width
    o_ref[...] = x + mod[5:6] * acc


def _ffn(x2, mod, gn, wup, cw, wdn, seq):
    t = x2.shape[0]
    tiles_per_seq = seq // TM
    r = TM // HALO
    nh = t // HALO
    row = lambda i: (i, 0)
    prev = lambda i: (jnp.maximum(i * r - 1, 0), 0)
    nxt = lambda i: (jnp.minimum((i + 1) * r, nh - 1), 0)
    chunk = max(FF_CHUNKS)
    return pl.pallas_call(
        functools.partial(_ffn_kernel, tiles_per_seq=tiles_per_seq),
        grid=(t // TM,),
        in_specs=[
            pl.BlockSpec((HALO, D_MODEL), prev), pl.BlockSpec((TM, D_MODEL), row), pl.BlockSpec((HALO, D_MODEL), nxt),
            pl.BlockSpec((None, 8, D_MODEL), lambda i: (i // tiles_per_seq, 0, 0)),
            _const_spec((1, D_MODEL)),
            _const_spec((D_MODEL, 2 * D_FF)),
            _const_spec((8, 2 * D_FF)),
            _const_spec((D_FF, D_MODEL)),
        ],
        out_specs=pl.BlockSpec((TM, D_MODEL), row),
        out_shape=jax.ShapeDtypeStruct((t, D_MODEL), F32),
        scratch_shapes=[pltpu.VMEM((TM + 2 * HALO, D_MODEL), BF16),
                        pltpu.VMEM((TM + 2 * HALO, chunk), F32), pltpu.VMEM((TM + 2 * HALO, chunk), F32)],
        compiler_params=_cparams(("parallel",)),
        name="ffn",
    )(x2, x2, x2, mod, gn, wup, cw, wdn)


def _swap_halves(a, head_dim):
    shp = a.shape
    a = a.reshape(shp[:-1] + (shp[-1] // head_dim, 2, head_dim // 2))
    return a[..., ::-1, :].reshape(shp)


def _rope_table(seq):
    pos = jnp.arange(seq, dtype=F32)[:, None]

    def one(dim):
        inv = 1.0 / (ROPE_THETA ** (jnp.arange(0, dim, 2, dtype=F32) / dim))
        ang = pos * inv[None, :]
        c, s = jnp.cos(ang), jnp.sin(ang)
        rep = 128 // dim
        return jnp.tile(jnp.concatenate([c, c], 1), (1, rep)), jnp.tile(jnp.concatenate([-s, s], 1), (1, rep))

    cc, sc = one(HEAD_C)
    cd, sd = one(HEAD_D)
    return jnp.concatenate([cc, sc, cd, sd], axis=1)


def _layer_params(l, w_in, conv_a, conv_b, conv_b_bias, ln_b_gain, ln_b_bias, g_q_c, g_k_c, sink_c, g_q_d, g_k_d,
                  g_subln_d, w_merge, w_branch, w_out, w_up, conv_ffn, conv_ffn_bias, w_down):
    w = w_in[l]
    w_ext = jnp.concatenate([
        w,
        _swap_halves(w[:, _CQ:_CK], HEAD_C), _swap_halves(w[:, _CK:_CV], HEAD_C),
        _swap_halves(w[:, _DQ:_DK], HEAD_D), _swap_halves(w[:, _DK:_DV], HEAD_D)], axis=1).astype(BF16)

    def tiled(g, scale):
        g = g.astype(F32) * scale
        rep = 256 // g.shape[0]
        return jnp.tile(g, rep), jnp.tile(_swap_halves(g, g.shape[0]), rep)

    rows = (tiled(g_q_c[l], HEAD_C ** -0.5 * LOG2E) + tiled(g_k_c[l], 1.0)
            + tiled(g_q_d[l], HEAD_D ** -0.5 * LOG2E) + tiled(g_k_d[l], 1.0))
    gains = jnp.stack(rows, axis=0)
    pad_rows = lambda a, n: jnp.pad(a.astype(F32), ((0, n - a.shape[0]), (0, 0)))
    return dict(
        w_ext=w_ext, gains=gains,
        conv_a=pad_rows(conv_a[l], 8), conv_b=pad_rows(conv_b[l], 32),
        bvec=pad_rows(jnp.stack([conv_b_bias[l], ln_b_gain[l], ln_b_bias[l]], 0), 8),
        sink=jnp.broadcast_to(pad_rows(sink_c[l][:, None] * LOG2E, 8), (8, 128)),
        gs=jnp.broadcast_to(g_subln_d[l].astype(F32)[:, None], (V_D, TQ_D)),
        wm=w_merge[l].astype(BF16), wb=w_branch[l].astype(BF16), wo=w_out[l].astype(BF16),
        wup=w_up[l].astype(BF16), wdn=w_down[l].astype(BF16),
        cw=pad_rows(jnp.concatenate([conv_ffn[l], conv_ffn_bias[l][None, :]], 0), 8),
    )


def _segment_ones():
    idx = jnp.arange(256)
    seg = lambda n: (idx[:, None] // n == idx[None, :] // n).astype(BF16)
    return jnp.stack([seg(HEAD_C), seg(HEAD_D)], 0)


def _trunk(x, mods, layers, seg, g_norm_mix, g_norm_ffn, lam_d):
    b, s, d = x.shape
    x2 = x.reshape(b * s, d)
    rope = _rope_table(s)
    for l in range(DEPTH):
        p = layers[l]
        gn1 = g_norm_mix[l].astype(F32)[None, :]
        gn2 = g_norm_ffn[l].astype(F32)[None, :]
        za, zb, qc, kc, vc, qd, kd, vd = _proj(x2, mods[l], gn1, p["w_ext"], rope, p["gains"], seg, s)
        ya, yb = _mix_ab(za, zb, p["conv_a"], p["conv_b"], p["bvec"], s)
        yc = _attn_c(qc, kc, vc, p["sink"], s)
        lam_init = 0.8 - 0.6 * math.exp(-0.3 * l)
        qt = qd.reshape(b, s, 256).transpose(0, 2, 1)
        kt = kd.reshape(b, s // TK_D, TK_D, 256)
        vt = vd.reshape(b, s // TK_D, TK_D, 256).transpose(0, 1, 3, 2)
        ydt = _attn_d(qt, kt, vt, lam_d[l].astype(F32), p["gs"], lam_init)
        yd = ydt.transpose(0, 2, 1).reshape(b * s, 256).astype(BF16)
        x2 = _merge(x2, mods[l], gn1, ya, yb, yc, yd, p["wm"], p["wb"], p["wo"], s)
        x2 = _ffn(x2, mods[l], gn2, p["wup"], p["cw"], p["wdn"], s)
    return x2.reshape(b, s, d)


def kernel(x_prompt, x_sample, c_prompt, c_sample, w_ada, b_ada, g_norm_mix, w_in, conv_a, conv_b, conv_b_bias,
           ln_b_gain, ln_b_bias, g_q_c, g_k_c, sink_c, g_q_d, g_k_d, lam_d, g_subln_d, w_merge, w_branch, w_out,
           g_norm_ffn, w_up, conv_ffn, conv_ffn_bias, w_down):
    bp, bs = x_prompt.shape[0], x_sample.shape[0]
    assert bp + bs <= 8
    c8 = jnp.concatenate([c_prompt, c_sample, jnp.zeros((8 - bp - bs, D_MODEL), F32)], axis=0)
    mod = _ada(c8, w_ada, b_ada)
    mod = mod.reshape(DEPTH, 8, N_ADA, D_MODEL)
    mod = jnp.pad(mod, ((0, 0), (0, 0), (0, 8 - N_ADA), (0, 0)))
    layers = [_layer_params(l, w_in, conv_a, conv_b, conv_b_bias, ln_b_gain, ln_b_bias, g_q_c, g_k_c, sink_c, g_q_d,
                            g_k_d, g_subln_d, w_merge, w_branch, w_out, w_up, conv_ffn, conv_ffn_bias, w_down)
              for l in range(DEPTH)]
    seg = _segment_ones()
    y_prompt = _trunk(x_prompt, [mod[l, :bp] for l in range(DEPTH)], layers, seg, g_norm_mix, g_norm_ffn, lam_d)
    y_sample = _trunk(x_sample, [mod[l, bp:bp + bs] for l in range(DEPTH)], layers, seg, g_norm_mix, g_norm_ffn,
                      lam_d)
    return (y_prompt, y_sample)
```

```python
import functools
import math

import jax
import jax.numpy as jnp
from jax import lax
from jax.experimental import pallas as pl
from jax.experimental.pallas import tpu as pltpu

F32 = jnp.float32
BF16 = jnp.bfloat16

D_MODEL = 1024
DEPTH = 2
BRANCH = 256
HEAD_C = 64
WINDOW = 128
HEAD_D = 32
V_D = 64
VA_D = 80
N_HEADS_D = 4
CONF_K = 31
D_FF = 2816
D_IN = 2560
N_ADA = 6
ROPE_THETA = 10000.0
EPS = 1e-6
NEG = -1e30
LOG2E = 1.4426950408889634

VMEM_LIMIT = 56 * 1024 * 1024

TM = 512
HALO = 16
TQ_C = 512
TQ_D = 256
TK_D = 256
FF_CHUNKS = (512, 512, 512, 512, 512, 256)


def _cparams(sem):
    return pltpu.CompilerParams(dimension_semantics=sem, vmem_limit_bytes=VMEM_LIMIT)


def _const_spec(shape):
    nd = len(shape)
    return pl.BlockSpec(shape, lambda *_: (0,) * nd, pipeline_mode=pl.Buffered(1))


def _modulated_norm(x, gain, shift, scale):
    ms = jnp.mean(x * x, axis=-1, keepdims=True)
    return x * lax.rsqrt(ms + EPS) * gain * (1.0 + scale) + shift


def _ada_kernel(c_ref, w_ref, b_ref, o_ref):
    c = c_ref[...]
    a = c * jax.nn.sigmoid(c)
    a_hi = a.astype(BF16)
    a_lo = (a - a_hi.astype(F32)).astype(BF16)
    w = w_ref[...]
    w_hi = w.astype(BF16)
    w_lo = (w - w_hi.astype(F32)).astype(BF16)
    acc = jnp.dot(a_hi, w_hi, preferred_element_type=F32)
    acc += jnp.dot(a_lo, w_hi, preferred_element_type=F32)
    acc += jnp.dot(a_hi, w_lo, preferred_element_type=F32)
    o_ref[...] = acc + b_ref[...]


def _ada(c8, w_ada, b_ada):
    tn = 1536
    n = N_ADA * D_MODEL
    return pl.pallas_call(
        _ada_kernel,
        grid=(DEPTH, n // tn),
        in_specs=[
            pl.BlockSpec((8, D_MODEL), lambda l, j: (0, 0)),
            pl.BlockSpec((None, D_MODEL, tn), lambda l, j: (l, 0, j)),
            pl.BlockSpec((None, 1, tn), lambda l, j: (l, 0, j)),
        ],
        out_specs=pl.BlockSpec((None, 8, tn), lambda l, j: (l, 0, j)),
        out_shape=jax.ShapeDtypeStruct((DEPTH, 8, n), F32),
        compiler_params=_cparams(("arbitrary", "arbitrary")),
        name="ada",
    )(c8, w_ada, b_ada.reshape(DEPTH, 1, n))


_A0, _B0, _CQ, _CK, _CV, _DQ, _DK, _DV = 0, 768, 1280, 1536, 1664, 1792, 2048, 2304
_CQS, _CKS, _DQS, _DKS, _WEXT = 2560, 2816, 2944, 3200, 3456


def _normed_rope(z, zsw, g, gsw, cos, sin, seg, inv_n):
    sq = z * z
    hi = sq.astype(BF16)
    lo = (sq - hi.astype(F32)).astype(BF16)
    ssq = jnp.dot(hi, seg, preferred_element_type=F32) + jnp.dot(lo, seg, preferred_element_type=F32)
    r = lax.rsqrt(ssq * inv_n + EPS)
    return r * (z * (g * cos) + zsw * (gsw * sin))


def _proj_kernel(x_ref, mod_ref, gn_ref, w_ref, rope_ref, gain_ref, seg_ref,
                 za_ref, zb_ref, qc_ref, kc_ref, vc_ref, qd_ref, kd_ref, vd_ref):
    mod = mod_ref[...]
    h = _modulated_norm(x_ref[...], gn_ref[...], mod[0:1], mod[1:2])
    hb = h.astype(BF16)

    def mm(c0, c1):
        return jnp.dot(hb, w_ref[:, c0:c1], preferred_element_type=F32)

    za_ref[...] = mm(_A0, _B0).astype(BF16)
    zb_ref[...] = mm(_B0, _CQ).astype(BF16)
    vc_ref[...] = mm(_CV, _DQ).astype(BF16)
    vd_ref[...] = mm(_DV, _CQS).astype(BF16)

    rope = rope_ref[...]
    cos_c, sin_c, cos_d, sin_d = (rope[:, 0:128], rope[:, 128:256], rope[:, 256:384], rope[:, 384:512])
    two = lambda t: jnp.concatenate([t, t], axis=1)
    gains = gain_ref[...]
    seg64 = seg_ref[0]
    seg32 = seg_ref[1]
    qc_ref[...] = _normed_rope(mm(_CQ, _CK), mm(_CQS, _CKS), gains[0:1], gains[1:2],
                               two(cos_c), two(sin_c), seg64, 1.0 / HEAD_C).astype(BF16)
    kc_ref[...] = _normed_rope(mm(_CK, _CV), mm(_CKS, _DQS), gains[2:3, 0:128], gains[3:4, 0:128],
                               cos_c, sin_c, seg64[0:128, 0:128], 1.0 / HEAD_C).astype(BF16)
    qd_ref[...] = _normed_rope(mm(_DQ, _DK), mm(_DQS, _DKS), gains[4:5], gains[5:6],
                               two(cos_d), two(sin_d), seg32, 1.0 / HEAD_D).astype(BF16)
    kd_ref[...] = _normed_rope(mm(_DK, _DV), mm(_DKS, _WEXT), gains[6:7], gains[7:8],
                               two(cos_d), two(sin_d), seg32, 1.0 / HEAD_D).astype(BF16)


def _proj(x2, mod, gn, w_ext, rope, gains, seg, seq):
    t = x2.shape[0]
    tiles_per_seq = seq // TM
    row = lambda i: (i, 0)
    widths = (768, 512, 256, 128, 128, 256, 256, 256)
    return pl.pallas_call(
        _proj_kernel,
        grid=(t // TM,),
        in_specs=[
            pl.BlockSpec((TM, D_MODEL), row),
            pl.BlockSpec((None, 8, D_MODEL), lambda i: (i // tiles_per_seq, 0, 0)),
            _const_spec((1, D_MODEL)),
            _const_spec((D_MODEL, _WEXT)),
            pl.BlockSpec((TM, 512), lambda i: (i % tiles_per_seq, 0)),
            _const_spec((8, 256)),
            _const_spec((2, 256, 256)),
        ],
        out_specs=[pl.BlockSpec((TM, w), row) for w in widths],
        out_shape=[jax.ShapeDtypeStruct((t, w), BF16) for w in widths],
        compiler_params=_cparams(("parallel",)),
        name="proj",
    )(x2, mod, gn, w_ext, rope, gains, seg)


def _mix_ab_kernel(zap_ref, za_ref, zan_ref, zbp_ref, zb_ref, zbn_ref, ca_ref, cb_ref, bvec_ref,
                   ya_ref, yb_ref, abuf, bbuf, *, tiles_per_seq):
    i = pl.program_id(0)
    pos = i % tiles_per_seq
    keep_prev = (pos != 0).astype(F32)
    keep_next = (pos != tiles_per_seq - 1).astype(F32)

    def a_in(z):
        z = z.astype(F32)
        return z[:, 256:512] * z[:, 512:768]

    def b_in(z):
        z = z.astype(F32)
        return z[:, 0:256] * jax.nn.sigmoid(z[:, 256:512])

    abuf[0:HALO, :] = a_in(zap_ref[...]) * keep_prev
    abuf[HALO:HALO + TM, :] = a_in(za_ref[...])
    abuf[HALO + TM:, :] = a_in(zan_ref[...]) * keep_next
    bbuf[0:HALO, :] = b_in(zbp_ref[...]) * keep_prev
    bbuf[HALO:HALO + TM, :] = b_in(zb_ref[...])
    bbuf[HALO + TM:, :] = b_in(zbn_ref[...]) * keep_next

    ca = ca_ref[...]
    conv_a = (ca[0:1] * abuf[HALO - 1:HALO - 1 + TM, :] + ca[1:2] * abuf[HALO:HALO + TM, :]
              + ca[2:3] * abuf[HALO + 1:HALO + 1 + TM, :])
    ya_ref[...] = (za_ref[:, 0:256].astype(F32) * conv_a).astype(BF16)

    cb = cb_ref[...]
    pad = (CONF_K - 1) // 2
    t = None
    for k in range(CONF_K):
        term = cb[k:k + 1] * bbuf[HALO - pad + k:HALO - pad + k + TM, :]
        t = term if t is None else t + term
    bvec = bvec_ref[...]
    t = t + bvec[0:1]
    mu = jnp.mean(t, axis=-1, keepdims=True)
    tc = t - mu
    var = jnp.mean(tc * tc, axis=-1, keepdims=True)
    y = tc * lax.rsqrt(var + EPS) * bvec[1:2] + bvec[2:3]
    yb_ref[...] = (y * jax.nn.sigmoid(y)).astype(BF16)


def _mix_ab(za, zb, conv_a, conv_b, bvec, seq):
    t = za.shape[0]
    tiles_per_seq = seq // TM
    r = TM // HALO
    nh = t // HALO
    row = lambda i: (i, 0)
    prev = lambda i: (jnp.maximum(i * r - 1, 0), 0)
    nxt = lambda i: (jnp.minimum((i + 1) * r, nh - 1), 0)
    return pl.pallas_call(
        functools.partial(_mix_ab_kernel, tiles_per_seq=tiles_per_seq),
        grid=(t // TM,),
        in_specs=[
            pl.BlockSpec((HALO, 768), prev), pl.BlockSpec((TM, 768), row), pl.BlockSpec((HALO, 768), nxt),
            pl.BlockSpec((HALO, 512), prev), pl.BlockSpec((TM, 512), row), pl.BlockSpec((HALO, 512), nxt),
            _const_spec((8, 256)), _const_spec((32, 256)), _const_spec((8, 256)),
        ],
        out_specs=[pl.BlockSpec((TM, BRANCH), row), pl.BlockSpec((TM, BRANCH), row)],
        out_shape=[jax.ShapeDtypeStruct((t, BRANCH), BF16)] * 2,
        scratch_shapes=[pltpu.VMEM((TM + 2 * HALO, BRANCH), F32), pltpu.VMEM((TM + 2 * HALO, BRANCH), F32)],
        compiler_params=_cparams(("parallel",)),
        name="mix_ab",
    )(za, za, za, zb, zb, zb, conv_a, conv_b, bvec)


def _attn_c_kernel(q_ref, kp_ref, k_ref, kn_ref, vp_ref, v_ref, vn_ref, sink_ref, o_ref, *, tiles_per_seq):
    i = pl.program_id(0)
    pos = i % tiles_per_seq
    first = pos == 0
    last = pos == tiles_per_seq - 1
    nsub = TQ_C // WINDOW

    def halves(prev, cur, nxt):
        x = jnp.concatenate([prev[...], cur[...], nxt[...]], axis=0).astype(F32)
        xr = pltpu.roll(x, 64, axis=1)
        lane = lax.broadcasted_iota(jnp.int32, x.shape, 1)
        lo = lane < 64
        zero = jnp.zeros_like(x)
        return [[jnp.where(lo, x, zero).astype(BF16), jnp.where(lo, zero, xr).astype(BF16)],
                [jnp.where(lo, xr, zero).astype(BF16), jnp.where(lo, zero, x).astype(BF16)]]

    kz = halves(kp_ref, k_ref, kn_ref)
    vz = halves(vp_ref, v_ref, vn_ref)
    sink = sink_ref[...]
    a = lax.broadcasted_iota(jnp.int32, (WINDOW, 3 * WINDOW), 0)
    tt = lax.broadcasted_iota(jnp.int32, (WINDOW, 3 * WINDOW), 1)
    rel_ok = jnp.abs(tt - WINDOW - a) <= WINDOW

    for j in range(nsub):
        ok = rel_ok
        if j == 0:
            ok = ok & (tt >= jnp.where(first, WINDOW, 0))
        if j == nsub - 1:
            ok = ok & (tt < jnp.where(last, 2 * WINDOW, 3 * WINDOW))
        r0 = j * WINDOW
        for hk in range(2):
            qp = q_ref[r0:r0 + WINDOW, hk * 128:(hk + 1) * 128]
            out = None
            for gi in range(2):
                kk = kz[hk][gi][r0:r0 + 3 * WINDOW, :]
                vv = vz[hk][gi][r0:r0 + 3 * WINDOW, :]
                s = lax.dot_general(qp, kk, (((1,), (1,)), ((), ())), preferred_element_type=F32)
                s = jnp.where(ok, s, NEG)
                sk = sink[2 * hk + gi:2 * hk + gi + 1, 0:1]
                m = jnp.maximum(jnp.max(s, axis=-1, keepdims=True), sk)
                p = jnp.exp2(s - m)
                den = jnp.sum(p, axis=-1, keepdims=True) + jnp.exp2(sk - m)
                pv = jnp.dot((p / den).astype(BF16), vv, preferred_element_type=F32)
                out = pv if out is None else out + pv
            o_ref[r0:r0 + WINDOW, hk * 128:(hk + 1) * 128] = out.astype(BF16)


def _attn_c(qc, kc, vc, sink, seq):
    t = qc.shape[0]
    tiles_per_seq = seq // TQ_C
    r = TQ_C // WINDOW
    nh = t // WINDOW
    row = lambda i: (i, 0)
    prev = lambda i: (jnp.maximum(i * r - 1, 0), 0)
    nxt = lambda i: (jnp.minimum((i + 1) * r, nh - 1), 0)
    kv_specs = [pl.BlockSpec((WINDOW, 128), prev), pl.BlockSpec((TQ_C, 128), row), pl.BlockSpec((WINDOW, 128), nxt)]
    return pl.pallas_call(
        functools.partial(_attn_c_kernel, tiles_per_seq=tiles_per_seq),
        grid=(t // TQ_C,),
        in_specs=[pl.BlockSpec((TQ_C, BRANCH), row)] + kv_specs + kv_specs + [_const_spec((8, 128))],
        out_specs=pl.BlockSpec((TQ_C, BRANCH), row),
        out_shape=jax.ShapeDtypeStruct((t, BRANCH), BF16),
        compiler_params=_cparams(("parallel",)),
        name="attn_c",
    )(qc, kc, kc, kc, vc, vc, vc, sink)


def _attn_d_kernel(lam_ref, gs_ref, qt_ref, k_ref, vt_ref, o_ref, s_ref, mx_ref, m_ref, acc_ref, *, nkt, lam_init):
    qt = qt_ref[...].astype(F32)
    grp = lax.broadcasted_iota(jnp.int32, (128, TQ_D), 0) // HEAD_D
    qz = []
    for g in range(8):
        half = qt[(g // 4) * 128:(g // 4 + 1) * 128, :]
        qz.append(jnp.where(grp == (g % 4), half, 0.0).astype(BF16))
    m_ref[...] = jnp.full(m_ref.shape, NEG, F32)
    acc_ref[...] = jnp.zeros(acc_ref.shape, F32)

    def scores(kt, slot):
        kk = k_ref[kt]
        for g in range(8):
            s = jnp.dot(kk[:, (g // 4) * 128:(g // 4 + 1) * 128], qz[g], preferred_element_type=F32)
            s_ref[slot, g] = s
            mx_ref[slot, g] = jnp.max(s.reshape(TK_D // 8, 8, TQ_D), axis=0)

    def consume(kt, slot):
        vt = vt_ref[kt]
        for g in range(8):
            h = g // 2
            m_old = m_ref[g]
            m_new = jnp.maximum(m_old, jnp.max(mx_ref[slot, g], axis=0, keepdims=True))
            alpha = jnp.exp2(m_old - m_new)
            p = jnp.exp2(s_ref[slot, g] - m_new[0:1])
            pv = jnp.dot(vt[h * VA_D:(h + 1) * VA_D, :], p.astype(BF16), preferred_element_type=F32)
            acc_ref[g] = acc_ref[g] * alpha[0:1] + pv
            m_ref[g] = m_new

    scores(0, 0)

    def body(u, carry):
        kt = 2 * u
        scores(kt + 1, 1)
        consume(kt, 0)
        scores(jnp.minimum(kt + 2, nkt - 1), 0)
        consume(kt + 1, 1)
        return carry

    lax.fori_loop(0, nkt // 2, body, 0)

    lf = lam_ref[...]
    lam = (jnp.exp(jnp.sum(lf[0:1] * lf[1:2], axis=-1, keepdims=True))
           - jnp.exp(jnp.sum(lf[2:3] * lf[3:4], axis=-1, keepdims=True)) + lam_init)
    gs = gs_ref[...]
    for h in range(N_HEADS_D):
        a1 = acc_ref[2 * h]
        a2 = acc_ref[2 * h + 1]
        o = a1[0:V_D] / a1[V_D:V_D + 1] - lam * (a2[0:V_D] / a2[V_D:V_D + 1])
        ms = jnp.mean(o * o, axis=0, keepdims=True)
        o_ref[h * V_D:(h + 1) * V_D, :] = o * lax.rsqrt(ms + EPS) * gs * (1.0 - lam_init)


def _attn_d(qd, kd, vd, lam_d, gs, lam_init):
    b, s, _ = qd.shape
    nkt = s // TK_D
    assert nkt % 2 == 0
    qt = qd.transpose(0, 2, 1)
    kt = kd.reshape(b, nkt, TK_D, 256)
    v4 = vd.reshape(b, nkt, TK_D, N_HEADS_D, V_D)
    ones = jnp.ones((b, nkt, TK_D, N_HEADS_D, 1), BF16)
    zeros = jnp.zeros((b, nkt, TK_D, N_HEADS_D, VA_D - V_D - 1), BF16)
    vt = jnp.concatenate([v4, ones, zeros], axis=-1).reshape(b, nkt, TK_D, N_HEADS_D * VA_D).transpose(0, 1, 3, 2)
    out = pl.pallas_call(
        functools.partial(_attn_d_kernel, nkt=nkt, lam_init=lam_init),
        grid=(b, s // TQ_D),
        in_specs=[
            _const_spec((4, HEAD_D)),
            _const_spec((V_D, TQ_D)),
            pl.BlockSpec((None, 256, TQ_D), lambda bi, qi: (bi, 0, qi)),
            pl.BlockSpec((None, nkt, TK_D, 256), lambda bi, qi: (bi, 0, 0, 0), pipeline_mode=pl.Buffered(1)),
            pl.BlockSpec((None, nkt, N_HEADS_D * VA_D, TK_D), lambda bi, qi: (bi, 0, 0, 0),
                         pipeline_mode=pl.Buffered(1)),
        ],
        out_specs=pl.BlockSpec((None, 256, TQ_D), lambda bi, qi: (bi, 0, qi)),
        out_shape=jax.ShapeDtypeStruct((b, 256, s), F32),
        scratch_shapes=[pltpu.VMEM((2, 8, TK_D, TQ_D), F32), pltpu.VMEM((2, 8, 8, TQ_D), F32),
                        pltpu.VMEM((8, 8, TQ_D), F32), pltpu.VMEM((8, VA_D, TQ_D), F32)],
        compiler_params=_cparams(("parallel", "arbitrary")),
        name="attn_d",
    )(lam_d, gs, qt, kt, vt)
    return out.transpose(0, 2, 1)


def _merge_kernel(x_ref, mod_ref, gn_ref, ya_ref, yb_ref, yc_ref, yd_ref, wm_ref, wb_ref, wo_ref, o_ref):
    x = x_ref[...]
    mod = mod_ref[...]
    hb = _modulated_norm(x, gn_ref[...], mod[0:1], mod[1:2]).astype(BF16)
    merged = None
    for i, y_ref in enumerate((ya_ref, yb_ref, yc_ref, yd_ref)):
        gate = jax.nn.sigmoid(jnp.dot(hb, wm_ref[i], preferred_element_type=F32))
        term = gate * jnp.dot(y_ref[...], wb_ref[i], preferred_element_type=F32)
        merged = term if merged is None else merged + term
    out = jnp.dot(merged.astype(BF16), wo_ref[...], preferred_element_type=F32)
    o_ref[...] = x + mod[2:3] * out


def _merge(x2, mod, gn, ya, yb, yc, yd, wm, wb, wo, seq):
    t = x2.shape[0]
    tiles_per_seq = seq // TM
    row = lambda i: (i, 0)
    return pl.pallas_call(
        _merge_kernel,
        grid=(t // TM,),
        in_specs=[
            pl.BlockSpec((TM, D_MODEL), row),
            pl.BlockSpec((None, 8, D_MODEL), lambda i: (i // tiles_per_seq, 0, 0)),
            _const_spec((1, D_MODEL)),
        ] + [pl.BlockSpec((TM, BRANCH), row)] * 4 + [
            _const_spec((4, D_MODEL, D_MODEL)),
            _const_spec((4, BRANCH, D_MODEL)),
            _const_spec((D_MODEL, D_MODEL)),
        ],
        out_specs=pl.BlockSpec((TM, D_MODEL), row),
        out_shape=jax.ShapeDtypeStruct((t, D_MODEL), F32),
        compiler_params=_cparams(("parallel",)),
        name="merge",
    )(x2, mod, gn, ya, yb, yc, yd, wm, wb, wo)


def _ffn_kernel(xp_ref, x_ref, xn_ref, mod_ref, gn_ref, wup_ref, cw_ref, wdn_ref, o_ref, hbuf, ug, uv,
                *, tiles_per_seq):
    i = pl.program_id(0)
    pos = i % tiles_per_seq
    keep_prev = (pos != 0).astype(F32)
    keep_next = (pos != tiles_per_seq - 1).astype(F32)
    mod = mod_ref[...]
    gn = gn_ref[...]
    x = x_ref[...]
    norm = lambda v: _modulated_norm(v, gn, mod[3:4], mod[4:5])
    hbuf[0:HALO, :] = (norm(xp_ref[...]) * keep_prev).astype(BF16)
    hbuf[HALO:HALO + TM, :] = norm(x).astype(BF16)
    hbuf[HALO + TM:, :] = (norm(xn_ref[...]) * keep_next).astype(BF16)
    hb = hbuf[...]

    def conv(buf, c0, width, col):
        w = cw_ref[:, col:col + width]
        return (w[0:1] * buf[HALO - 1:HALO - 1 + TM, 0:width] + w[1:2] * buf[HALO:HALO + TM, 0:width]
                + w[2:3] * buf[HALO + 1:HALO + 1 + TM, 0:width] + w[3:4])

    acc = None
    c0 = 0
    for width in FF_CHUNKS:
        ug[:, 0:width] = jnp.dot(hb, wup_ref[:, c0:c0 + width], preferred_element_type=F32)
        uv[:, 0:width] = jnp.dot(hb, wup_ref[:, D_FF + c0:D_FF + c0 + width], preferred_element_type=F32)
        g = conv(ug, c0, width, c0)
        v = conv(uv, c0, width, D_FF + c0)
        act = (g * jax.nn.sigmoid(g) * v).astype(BF16)
        part = jnp.dot(act, wdn_ref[c0:c0 + width, :], preferred_element_type=F32)
        acc = part if acc is None else acc + part
        c0 += width
    o_ref[...] = x + mod[5:6] * acc


def _ffn(x2, mod, gn, wup, cw, wdn, seq):
    t = x2.shape[0]
    tiles_per_seq = seq // TM
    r = TM // HALO
    nh = t // HALO
    row = lambda i: (i, 0)
    prev = lambda i: (jnp.maximum(i * r - 1, 0), 0)
    nxt = lambda i: (jnp.minimum((i + 1) * r, nh - 1), 0)
    chunk = max(FF_CHUNKS)
    return pl.pallas_call(
        functools.partial(_ffn_kernel, tiles_per_seq=tiles_per_seq),
        grid=(t // TM,),
        in_specs=[
            pl.BlockSpec((HALO, D_MODEL), prev), pl.BlockSpec((TM, D_MODEL), row), pl.BlockSpec((HALO, D_MODEL), nxt),
            pl.BlockSpec((None, 8, D_MODEL), lambda i: (i // tiles_per_seq, 0, 0)),
            _const_spec((1, D_MODEL)),
            _const_spec((D_MODEL, 2 * D_FF)),
            _const_spec((8, 2 * D_FF)),
            _const_spec((D_FF, D_MODEL)),
        ],
        out_specs=pl.BlockSpec((TM, D_MODEL), row),
        out_shape=jax.ShapeDtypeStruct((t, D_MODEL), F32),
        scratch_shapes=[pltpu.VMEM((TM + 2 * HALO, D_MODEL), BF16),
                        pltpu.VMEM((TM + 2 * HALO, chunk), F32), pltpu.VMEM((TM + 2 * HALO, chunk), F32)],
        compiler_params=_cparams(("parallel",)),
        name="ffn",
    )(x2, x2, x2, mod, gn, wup, cw, wdn)


def _swap_halves(a, head_dim):
    shp = a.shape
    a = a.reshape(shp[:-1] + (shp[-1] // head_dim, 2, head_dim // 2))
    return a[..., ::-1, :].reshape(shp)


def _rope_table(seq):
    pos = jnp.arange(seq, dtype=F32)[:, None]

    def one(dim):
        inv = 1.0 / (ROPE_THETA ** (jnp.arange(0, dim, 2, dtype=F32) / dim))
        ang = pos * inv[None, :]
        c, s = jnp.cos(ang), jnp.sin(ang)
        rep = 128 // dim
        return jnp.tile(jnp.concatenate([c, c], 1), (1, rep)), jnp.tile(jnp.concatenate([-s, s], 1), (1, rep))

    cc, sc = one(HEAD_C)
    cd, sd = one(HEAD_D)
    return jnp.concatenate([cc, sc, cd, sd], axis=1)


def _layer_params(l, w_in, conv_a, conv_b, conv_b_bias, ln_b_gain, ln_b_bias, g_q_c, g_k_c, sink_c, g_q_d, g_k_d,
                  g_subln_d, w_merge, w_branch, w_out, w_up, conv_ffn, conv_ffn_bias, w_down):
    w = w_in[l]
    w_ext = jnp.concatenate([
        w,
        _swap_halves(w[:, _CQ:_CK], HEAD_C), _swap_halves(w[:, _CK:_CV], HEAD_C),
        _swap_halves(w[:, _DQ:_DK], HEAD_D), _swap_halves(w[:, _DK:_DV], HEAD_D)], axis=1).astype(BF16)

    def tiled(g, scale):
        g = g.astype(F32) * scale
        rep = 256 // g.shape[0]
        return jnp.tile(g, rep), jnp.tile(_swap_halves(g, g.shape[0]), rep)

    rows = (tiled(g_q_c[l], HEAD_C ** -0.5 * LOG2E) + tiled(g_k_c[l], 1.0)
            + tiled(g_q_d[l], HEAD_D ** -0.5 * LOG2E) + tiled(g_k_d[l], 1.0))
    gains = jnp.stack(rows, axis=0)
    pad_rows = lambda a, n: jnp.pad(a.astype(F32), ((0, n - a.shape[0]), (0, 0)))
    return dict(
        w_ext=w_ext, gains=gains,
        conv_a=pad_rows(conv_a[l], 8), conv_b=pad_rows(conv_b[l], 32),
        bvec=pad_rows(jnp.stack([conv_b_bias[l], ln_b_gain[l], ln_b_bias[l]], 0), 8),
        sink=jnp.broadcast_to(pad_rows(sink_c[l][:, None] * LOG2E, 8), (8, 128)),
        gs=jnp.broadcast_to(g_subln_d[l].astype(F32)[:, None], (V_D, TQ_D)),
        wm=w_merge[l].astype(BF16), wb=w_branch[l].astype(BF16), wo=w_out[l].astype(BF16),
        wup=w_up[l].astype(BF16), wdn=w_down[l].astype(BF16),
        cw=pad_rows(jnp.concatenate([conv_ffn[l], conv_ffn_bias[l][None, :]], 0), 8),
    )


def _segment_ones():
    idx = jnp.arange(256)
    seg = lambda n: (idx[:, None] // n == idx[None, :] // n).astype(BF16)
    return jnp.stack([seg(HEAD_C), seg(HEAD_D)], 0)


def _trunk(x, mods, layers, seg, g_norm_mix, g_norm_ffn, lam_d):
    b, s, d = x.shape
    x2 = x.reshape(b * s, d)
    rope = _rope_table(s)
    for l in range(DEPTH):
        p = layers[l]
        gn1 = g_norm_mix[l].astype(F32)[None, :]
        gn2 = g_norm_ffn[l].astype(F32)[None, :]
        za, zb, qc, kc, vc, qd, kd, vd = _proj(x2, mods[l], gn1, p["w_ext"], rope, p["gains"], seg, s)
        ya, yb = _mix_ab(za, zb, p["conv_a"], p["conv_b"], p["bvec"], s)
        yc = _attn_c(qc, kc, vc, p["sink"], s)
        lam_init = 0.8 - 0.6 * math.exp(-0.3 * l)
        yd = _attn_d(qd.reshape(b, s, 256), kd.reshape(b, s, 256), vd.reshape(b, s, 256), lam_d[l].astype(F32),
                     p["gs"], lam_init).reshape(b * s, 256).astype(BF16)
        x2 = _merge(x2, mods[l], gn1, ya, yb, yc, yd, p["wm"], p["wb"], p["wo"], s)
        x2 = _ffn(x2, mods[l], gn2, p["wup"], p["cw"], p["wdn"], s)
    return x2.reshape(b, s, d)


def kernel(x_prompt, x_sample, c_prompt, c_sample, w_ada, b_ada, g_norm_mix, w_in, conv_a, conv_b, conv_b_bias,
           ln_b_gain, ln_b_bias, g_q_c, g_k_c, sink_c, g_q_d, g_k_d, lam_d, g_subln_d, w_merge, w_branch, w_out,
           g_norm_ffn, w_up, conv_ffn, conv_ffn_bias, w_down):
    bp, bs = x_prompt.shape[0], x_sample.shape[0]
    assert bp + bs <= 8
    c8 = jnp.concatenate([c_prompt, c_sample, jnp.zeros((8 - bp - bs, D_MODEL), F32)], axis=0)
    mod = _ada(c8, w_ada, b_ada)
    mod = mod.reshape(DEPTH, 8, N_ADA, D_MODEL)
    mod = jnp.pad(mod, ((0, 0), (0, 0), (0, 8 - N_ADA), (0, 0)))
    layers = [_layer_params(l, w_in, conv_a, conv_b, conv_b_bias, ln_b_gain, ln_b_bias, g_q_c, g_k_c, sink_c, g_q_d,
                            g_k_d, g_subln_d, w_merge, w_branch, w_out, w_up, conv_ffn, conv_ffn_bias, w_down)
              for l in range(DEPTH)]
    seg = _segment_ones()
    y_prompt = _trunk(x_prompt, [mod[l, :bp] for l in range(DEPTH)], layers, seg, g_norm_mix, g_norm_ffn, lam_d)
    y_sample = _trunk(x_sample, [mod[l, bp:bp + bs] for l in range(DEPTH)], layers, seg, g_norm_mix, g_norm_ffn,
                      lam_d)
    return (y_prompt, y_sample)
```

```python
import functools
import math

import jax
import jax.numpy as jnp
from jax import lax
from jax.experimental import pallas as pl
from jax.experimental.pallas import tpu as pltpu

F32 = jnp.float32
BF16 = jnp.bfloat16

D_MODEL = 1024
DEPTH = 2
BRANCH = 256
HEAD_C = 64
WINDOW = 128
HEAD_D = 32
V_D = 64
VA_D = 80
N_HEADS_D = 4
CONF_K = 31
D_FF = 2816
D_IN = 2560
N_ADA = 6
ROPE_THETA = 10000.0
EPS = 1e-6
NEG = -1e30
LOG2E = 1.4426950408889634

VMEM_LIMIT = 56 * 1024 * 1024

TM = 512
HALO = 16
TQ_C = 512
TQ_D = 256
TK_D = 256
UNROLL_D = 8
FF_CHUNKS = (512, 512, 512, 512, 512, 256)
FF_ROWS = 128


def _cparams(sem):
    return pltpu.CompilerParams(dimension_semantics=sem, vmem_limit_bytes=VMEM_LIMIT)


def _const_spec(shape):
    nd = len(shape)
    return pl.BlockSpec(shape, lambda *_: (0,) * nd, pipeline_mode=pl.Buffered(1))


def _modulated_norm(x, gain, shift, scale):
    ms = jnp.mean(x * x, axis=-1, keepdims=True)
    return x * lax.rsqrt(ms + EPS) * gain * (1.0 + scale) + shift


def _ada_kernel(c_ref, w_ref, b_ref, o_ref):
    c = c_ref[...]
    a = c * jax.nn.sigmoid(c)
    a_hi = a.astype(BF16)
    a_lo = (a - a_hi.astype(F32)).astype(BF16)
    w = w_ref[...]
    w_hi = w.astype(BF16)
    w_lo = (w - w_hi.astype(F32)).astype(BF16)
    acc = jnp.dot(a_hi, w_hi, preferred_element_type=F32)
    acc += jnp.dot(a_lo, w_hi, preferred_element_type=F32)
    acc += jnp.dot(a_hi, w_lo, preferred_element_type=F32)
    o_ref[...] = acc + b_ref[...]


def _ada(c8, w_ada, b_ada):
    tn = 1536
    n = N_ADA * D_MODEL
    return pl.pallas_call(
        _ada_kernel,
        grid=(DEPTH, n // tn),
        in_specs=[
            pl.BlockSpec((8, D_MODEL), lambda l, j: (0, 0)),
            pl.BlockSpec((None, D_MODEL, tn), lambda l, j: (l, 0, j)),
            pl.BlockSpec((None, 1, tn), lambda l, j: (l, 0, j)),
        ],
        out_specs=pl.BlockSpec((None, 8, tn), lambda l, j: (l, 0, j)),
        out_shape=jax.ShapeDtypeStruct((DEPTH, 8, n), F32),
        compiler_params=_cparams(("arbitrary", "arbitrary")),
        name="ada",
    )(c8, w_ada, b_ada.reshape(DEPTH, 1, n))


_A0, _B0, _CQ, _CK, _CV, _DQ, _DK, _DV = 0, 768, 1280, 1536, 1664, 1792, 2048, 2304
_CQS, _CKS, _DQS, _DKS, _WEXT = 2560, 2816, 2944, 3200, 3456


def _normed_rope(z, zsw, g, gsw, cos, sin, seg, inv_n):
    sq = z * z
    hi = sq.astype(BF16)
    lo = (sq - hi.astype(F32)).astype(BF16)
    ssq = jnp.dot(hi, seg, preferred_element_type=F32) + jnp.dot(lo, seg, preferred_element_type=F32)
    r = lax.rsqrt(ssq * inv_n + EPS)
    return r * (z * (g * cos) + zsw * (gsw * sin))


def _proj_kernel(x_ref, mod_ref, gn_ref, w_ref, rope_ref, gain_ref, seg_ref,
                 za_ref, zb_ref, qc_ref, kc_ref, vc_ref, qd_ref, kd_ref, vd_ref):
    mod = mod_ref[...]
    h = _modulated_norm(x_ref[...], gn_ref[...], mod[0:1], mod[1:2])
    hb = h.astype(BF16)

    def mm(c0, c1):
        return jnp.dot(hb, w_ref[:, c0:c1], preferred_element_type=F32)

    za_ref[...] = mm(_A0, _B0).astype(BF16)
    zb_ref[...] = mm(_B0, _CQ).astype(BF16)
    vc_ref[...] = mm(_CV, _DQ).astype(BF16)
    vd_ref[...] = mm(_DV, _CQS).astype(BF16)

    rope = rope_ref[...]
    cos_c, sin_c, cos_d, sin_d = (rope[:, 0:128], rope[:, 128:256], rope[:, 256:384], rope[:, 384:512])
    two = lambda t: jnp.concatenate([t, t], axis=1)
    gains = gain_ref[...]
    seg64 = seg_ref[0]
    seg32 = seg_ref[1]
    qc_ref[...] = _normed_rope(mm(_CQ, _CK), mm(_CQS, _CKS), gains[0:1], gains[1:2],
                               two(cos_c), two(sin_c), seg64, 1.0 / HEAD_C).astype(BF16)
    kc_ref[...] = _normed_rope(mm(_CK, _CV), mm(_CKS, _DQS), gains[2:3, 0:128], gains[3:4, 0:128],
                               cos_c, sin_c, seg64[0:128, 0:128], 1.0 / HEAD_C).astype(BF16)
    qd_ref[...] = _normed_rope(mm(_DQ, _DK), mm(_DQS, _DKS), gains[4:5], gains[5:6],
                               two(cos_d), two(sin_d), seg32, 1.0 / HEAD_D).astype(BF16)
    kd_ref[...] = _normed_rope(mm(_DK, _DV), mm(_DKS, _WEXT), gains[6:7], gains[7:8],
                               two(cos_d), two(sin_d), seg32, 1.0 / HEAD_D).astype(BF16)


def _proj(x2, mod, gn, w_ext, rope, gains, seg, seq):
    t = x2.shape[0]
    tiles_per_seq = seq // TM
    row = lambda i: (i, 0)
    widths = (768, 512, 256, 128, 128, 256, 256, 256)
    return pl.pallas_call(
        _proj_kernel,
        grid=(t // TM,),
        in_specs=[
            pl.BlockSpec((TM, D_MODEL), row),
            pl.BlockSpec((None, 8, D_MODEL), lambda i: (i // tiles_per_seq, 0, 0)),
            _const_spec((1, D_MODEL)),
            _const_spec((D_MODEL, _WEXT)),
            pl.BlockSpec((TM, 512), lambda i: (i % tiles_per_seq, 0)),
            _const_spec((8, 256)),
            _const_spec((2, 256, 256)),
        ],
        out_specs=[pl.BlockSpec((TM, w), row) for w in widths],
        out_shape=[jax.ShapeDtypeStruct((t, w), BF16) for w in widths],
        compiler_params=_cparams(("parallel",)),
        name="proj",
    )(x2, mod, gn, w_ext, rope, gains, seg)


def _mix_ab_kernel(zap_ref, za_ref, zan_ref, zbp_ref, zb_ref, zbn_ref, ca_ref, cb_ref, bvec_ref,
                   ya_ref, yb_ref, abuf, bbuf, *, tiles_per_seq):
    i = pl.program_id(0)
    pos = i % tiles_per_seq
    keep_prev = (pos != 0).astype(F32)
    keep_next = (pos != tiles_per_seq - 1).astype(F32)

    def a_in(z):
        z = z.astype(F32)
        return z[:, 256:512] * z[:, 512:768]

    def b_in(z):
        z = z.astype(F32)
        return z[:, 0:256] * jax.nn.sigmoid(z[:, 256:512])

    abuf[0:HALO, :] = a_in(zap_ref[...]) * keep_prev
    abuf[HALO:HALO + TM, :] = a_in(za_ref[...])
    abuf[HALO + TM:, :] = a_in(zan_ref[...]) * keep_next
    bbuf[0:HALO, :] = b_in(zbp_ref[...]) * keep_prev
    bbuf[HALO:HALO + TM, :] = b_in(zb_ref[...])
    bbuf[HALO + TM:, :] = b_in(zbn_ref[...]) * keep_next

    ca = ca_ref[...]
    conv_a = (ca[0:1] * abuf[HALO - 1:HALO - 1 + TM, :] + ca[1:2] * abuf[HALO:HALO + TM, :]
              + ca[2:3] * abuf[HALO + 1:HALO + 1 + TM, :])
    ya_ref[...] = (za_ref[:, 0:256].astype(F32) * conv_a).astype(BF16)

    cb = cb_ref[...]
    pad = (CONF_K - 1) // 2
    t = None
    for k in range(CONF_K):
        term = cb[k:k + 1] * bbuf[HALO - pad + k:HALO - pad + k + TM, :]
        t = term if t is None else t + term
    bvec = bvec_ref[...]
    t = t + bvec[0:1]
    mu = jnp.mean(t, axis=-1, keepdims=True)
    tc = t - mu
    var = jnp.mean(tc * tc, axis=-1, keepdims=True)
    y = tc * lax.rsqrt(var + EPS) * bvec[1:2] + bvec[2:3]
    yb_ref[...] = (y * jax.nn.sigmoid(y)).astype(BF16)


def _mix_ab(za, zb, conv_a, conv_b, bvec, seq):
    t = za.shape[0]
    tiles_per_seq = seq // TM
    r = TM // HALO
    nh = t // HALO
    row = lambda i: (i, 0)
    prev = lambda i: (jnp.maximum(i * r - 1, 0), 0)
    nxt = lambda i: (jnp.minimum((i + 1) * r, nh - 1), 0)
    return pl.pallas_call(
        functools.partial(_mix_ab_kernel, tiles_per_seq=tiles_per_seq),
        grid=(t // TM,),
        in_specs=[
            pl.BlockSpec((HALO, 768), prev), pl.BlockSpec((TM, 768), row), pl.BlockSpec((HALO, 768), nxt),
            pl.BlockSpec((HALO, 512), prev), pl.BlockSpec((TM, 512), row), pl.BlockSpec((HALO, 512), nxt),
            _const_spec((8, 256)), _const_spec((32, 256)), _const_spec((8, 256)),
        ],
        out_specs=[pl.BlockSpec((TM, BRANCH), row), pl.BlockSpec((TM, BRANCH), row)],
        out_shape=[jax.ShapeDtypeStruct((t, BRANCH), BF16)] * 2,
        scratch_shapes=[pltpu.VMEM((TM + 2 * HALO, BRANCH), F32), pltpu.VMEM((TM + 2 * HALO, BRANCH), F32)],
        compiler_params=_cparams(("parallel",)),
        name="mix_ab",
    )(za, za, za, zb, zb, zb, conv_a, conv_b, bvec)


def _attn_c_kernel(q_ref, kp_ref, k_ref, kn_ref, vp_ref, v_ref, vn_ref, sink_ref, o_ref, s_ref, p_ref,
                   *, tiles_per_seq):
    i = pl.program_id(0)
    pos = i % tiles_per_seq
    first = pos == 0
    last = pos == tiles_per_seq - 1
    nsub = TQ_C // WINDOW

    def halves(prev, cur, nxt):
        x = jnp.concatenate([prev[...], cur[...], nxt[...]], axis=0).astype(F32)
        xr = pltpu.roll(x, 64, axis=1)
        lane = lax.broadcasted_iota(jnp.int32, x.shape, 1)
        lo = lane < 64
        zero = jnp.zeros_like(x)
        return [[jnp.where(lo, x, zero).astype(BF16), jnp.where(lo, zero, xr).astype(BF16)],
                [jnp.where(lo, xr, zero).astype(BF16), jnp.where(lo, zero, x).astype(BF16)]]

    kz = halves(kp_ref, k_ref, kn_ref)
    vz = halves(vp_ref, v_ref, vn_ref)
    sink = sink_ref[...]
    a = lax.broadcasted_iota(jnp.int32, (WINDOW, 3 * WINDOW), 0)
    tt = lax.broadcasted_iota(jnp.int32, (WINDOW, 3 * WINDOW), 1)
    rel_ok = jnp.abs(tt - WINDOW - a) <= WINDOW

    problems = [(j, hk, gi) for j in range(nsub) for hk in range(2) for gi in range(2)]
    for n, (j, hk, gi) in enumerate(problems):
        r0 = j * WINDOW
        qp = q_ref[r0:r0 + WINDOW, hk * 128:(hk + 1) * 128]
        s_ref[n] = lax.dot_general(qp, kz[hk][gi][r0:r0 + 3 * WINDOW, :], (((1,), (1,)), ((), ())),
                                   preferred_element_type=F32)
    for n, (j, hk, gi) in enumerate(problems):
        ok = rel_ok
        if j == 0:
            ok = ok & (tt >= jnp.where(first, WINDOW, 0))
        if j == nsub - 1:
            ok = ok & (tt < jnp.where(last, 2 * WINDOW, 3 * WINDOW))
        s = jnp.where(ok, s_ref[n], NEG)
        sk = sink[2 * hk + gi:2 * hk + gi + 1, 0:1]
        m = jnp.maximum(jnp.max(s, axis=-1, keepdims=True), sk)
        p = jnp.exp2(s - m)
        den = jnp.sum(p, axis=-1, keepdims=True) + jnp.exp2(sk - m)
        p_ref[n] = (p / den).astype(BF16)
    for j in range(nsub):
        r0 = j * WINDOW
        for hk in range(2):
            n = (j * 2 + hk) * 2
            out = (jnp.dot(p_ref[n], vz[hk][0][r0:r0 + 3 * WINDOW, :], preferred_element_type=F32)
                   + jnp.dot(p_ref[n + 1], vz[hk][1][r0:r0 + 3 * WINDOW, :], preferred_element_type=F32))
            o_ref[r0:r0 + WINDOW, hk * 128:(hk + 1) * 128] = out.astype(BF16)


def _attn_c(qc, kc, vc, sink, seq):
    t = qc.shape[0]
    tiles_per_seq = seq // TQ_C
    r = TQ_C // WINDOW
    nh = t // WINDOW
    row = lambda i: (i, 0)
    prev = lambda i: (jnp.maximum(i * r - 1, 0), 0)
    nxt = lambda i: (jnp.minimum((i + 1) * r, nh - 1), 0)
    kv_specs = [pl.BlockSpec((WINDOW, 128), prev), pl.BlockSpec((TQ_C, 128), row), pl.BlockSpec((WINDOW, 128), nxt)]
    return pl.pallas_call(
        functools.partial(_attn_c_kernel, tiles_per_seq=tiles_per_seq),
        grid=(t // TQ_C,),
        in_specs=[pl.BlockSpec((TQ_C, BRANCH), row)] + kv_specs + kv_specs + [_const_spec((8, 128))],
        out_specs=pl.BlockSpec((TQ_C, BRANCH), row),
        out_shape=jax.ShapeDtypeStruct((t, BRANCH), BF16),
        scratch_shapes=[pltpu.VMEM((4 * r, WINDOW, 3 * WINDOW), F32), pltpu.VMEM((4 * r, WINDOW, 3 * WINDOW), BF16)],
        compiler_params=_cparams(("parallel",)),
        name="attn_c",
    )(qc, kc, kc, kc, vc, vc, vc, sink)


def _attn_d_kernel(lam_ref, gs_ref, qt_ref, k_ref, vt_ref, o_ref, s_ref, mx_ref, m_ref, acc_ref, *, nkt, lam_init):
    qt = qt_ref[...].astype(F32)
    grp = lax.broadcasted_iota(jnp.int32, (128, TQ_D), 0) // HEAD_D
    qz = []
    for g in range(8):
        half = qt[(g // 4) * 128:(g // 4 + 1) * 128, :]
        qz.append(jnp.where(grp == (g % 4), half, 0.0).astype(BF16))
    m_ref[...] = jnp.full(m_ref.shape, NEG, F32)
    acc_ref[...] = jnp.zeros(acc_ref.shape, F32)

    def scores(kk, slot, g):
        s = jnp.dot(kk[:, (g // 4) * 128:(g // 4 + 1) * 128], qz[g], preferred_element_type=F32)
        s_ref[slot, g] = s
        mx_ref[slot, g] = jnp.max(s.reshape(TK_D // 8, 8, TQ_D), axis=0)

    def consume(vt, slot, g):
        h = g // 2
        m_old = m_ref[g]
        m_new = jnp.maximum(m_old, jnp.max(mx_ref[slot, g], axis=0, keepdims=True))
        alpha = jnp.exp2(m_old - m_new)
        p = jnp.exp2(s_ref[slot, g] - m_new[0:1])
        pv = jnp.dot(vt[h * VA_D:(h + 1) * VA_D, :], p.astype(BF16), preferred_element_type=F32)
        acc_ref[g] = acc_ref[g] * alpha[0:1] + pv
        m_ref[g] = m_new

    kk0 = k_ref[0]
    for g in range(8):
        scores(kk0, 0, g)

    def body(u, carry):
        for j in range(UNROLL_D):
            kt = UNROLL_D * u + j
            kk = k_ref[jnp.minimum(kt + 1, nkt - 1)]
            vt = vt_ref[kt]
            for g in range(8):
                scores(kk, (j + 1) % 2, g)
                consume(vt, j % 2, g)
        return carry

    lax.fori_loop(0, nkt // UNROLL_D, body, 0)

    lf = lam_ref[...]
    lam = (jnp.exp(jnp.sum(lf[0:1] * lf[1:2], axis=-1, keepdims=True))
           - jnp.exp(jnp.sum(lf[2:3] * lf[3:4], axis=-1, keepdims=True)) + lam_init)
    gs = gs_ref[...]
    for h in range(N_HEADS_D):
        a1 = acc_ref[2 * h]
        a2 = acc_ref[2 * h + 1]
        o = a1[0:V_D] / a1[V_D:V_D + 1] - lam * (a2[0:V_D] / a2[V_D:V_D + 1])
        ms = jnp.mean(o * o, axis=0, keepdims=True)
        o_ref[h * V_D:(h + 1) * V_D, :] = o * lax.rsqrt(ms + EPS) * gs * (1.0 - lam_init)


def _attn_d(qd, kd, vd, lam_d, gs, lam_init):
    b, s, _ = qd.shape
    nkt = s // TK_D
    assert nkt % UNROLL_D == 0 and UNROLL_D % 2 == 0
    qt = qd.transpose(0, 2, 1)
    kt = kd.reshape(b, nkt, TK_D, 256)
    v4 = vd.reshape(b, nkt, TK_D, N_HEADS_D, V_D)
    ones = jnp.ones((b, nkt, TK_D, N_HEADS_D, 1), BF16)
    zeros = jnp.zeros((b, nkt, TK_D, N_HEADS_D, VA_D - V_D - 1), BF16)
    vt = jnp.concatenate([v4, ones, zeros], axis=-1).reshape(b, nkt, TK_D, N_HEADS_D * VA_D).transpose(0, 1, 3, 2)
    out = pl.pallas_call(
        functools.partial(_attn_d_kernel, nkt=nkt, lam_init=lam_init),
        grid=(b, s // TQ_D),
        in_specs=[
            _const_spec((4, HEAD_D)),
            _const_spec((V_D, TQ_D)),
            pl.BlockSpec((None, 256, TQ_D), lambda bi, qi: (bi, 0, qi)),
            pl.BlockSpec((None, nkt, TK_D, 256), lambda bi, qi: (bi, 0, 0, 0), pipeline_mode=pl.Buffered(1)),
            pl.BlockSpec((None, nkt, N_HEADS_D * VA_D, TK_D), lambda bi, qi: (bi, 0, 0, 0),
                         pipeline_mode=pl.Buffered(1)),
        ],
        out_specs=pl.BlockSpec((None, 256, TQ_D), lambda bi, qi: (bi, 0, qi)),
        out_shape=jax.ShapeDtypeStruct((b, 256, s), F32),
        scratch_shapes=[pltpu.VMEM((2, 8, TK_D, TQ_D), F32), pltpu.VMEM((2, 8, 8, TQ_D), F32),
                        pltpu.VMEM((8, 8, TQ_D), F32), pltpu.VMEM((8, VA_D, TQ_D), F32)],
        compiler_params=_cparams(("parallel", "arbitrary")),
        name="attn_d",
    )(lam_d, gs, qt, kt, vt)
    return out.transpose(0, 2, 1)


def _merge_kernel(x_ref, mod_ref, gn_ref, ya_ref, yb_ref, yc_ref, yd_ref, wm_ref, wb_ref, wo_ref, o_ref):
    x = x_ref[...]
    mod = mod_ref[...]
    hb = _modulated_norm(x, gn_ref[...], mod[0:1], mod[1:2]).astype(BF16)
    merged = None
    for i, y_ref in enumerate((ya_ref, yb_ref, yc_ref, yd_ref)):
        gate = jax.nn.sigmoid(jnp.dot(hb, wm_ref[i], preferred_element_type=F32))
        term = gate * jnp.dot(y_ref[...], wb_ref[i], preferred_element_type=F32)
        merged = term if merged is None else merged + term
    out = jnp.dot(merged.astype(BF16), wo_ref[...], preferred_element_type=F32)
    o_ref[...] = x + mod[2:3] * out


def _merge(x2, mod, gn, ya, yb, yc, yd, wm, wb, wo, seq):
    t = x2.shape[0]
    tiles_per_seq = seq // TM
    row = lambda i: (i, 0)
    return pl.pallas_call(
        _merge_kernel,
        grid=(t // TM,),
        in_specs=[
            pl.BlockSpec((TM, D_MODEL), row),
            pl.BlockSpec((None, 8, D_MODEL), lambda i: (i // tiles_per_seq, 0, 0)),
            _const_spec((1, D_MODEL)),
        ] + [pl.BlockSpec((TM, BRANCH), row)] * 4 + [
            _const_spec((4, D_MODEL, D_MODEL)),
            _const_spec((4, BRANCH, D_MODEL)),
            _const_spec((D_MODEL, D_MODEL)),
        ],
        out_specs=pl.BlockSpec((TM, D_MODEL), row),
        out_shape=jax.ShapeDtypeStruct((t, D_MODEL), F32),
        compiler_params=_cparams(("parallel",)),
        name="merge",
    )(x2, mod, gn, ya, yb, yc, yd, wm, wb, wo)


def _ffn_kernel(xp_ref, x_ref, xn_ref, mod_ref, gn_ref, wup_ref, cw_ref, wdn_ref, o_ref, hbuf, ug, uv, act, acc,
                *, tiles_per_seq):
    i = pl.program_id(0)
    pos = i % tiles_per_seq
    keep_prev = (pos != 0).astype(F32)
    keep_next = (pos != tiles_per_seq - 1).astype(F32)
    mod = mod_ref[...]
    gn = gn_ref[...]
    x = x_ref[...]
    norm = lambda v: _modulated_norm(v, gn, mod[3:4], mod[4:5])
    hbuf[0:HALO, :] = (norm(xp_ref[...]) * keep_prev).astype(BF16)
    hbuf[HALO:HALO + TM, :] = norm(x).astype(BF16)
    hbuf[HALO + TM:, :] = (norm(xn_ref[...]) * keep_next).astype(BF16)
    hb = hbuf[...]

    def conv(buf, slot, r0, width, col):
        w = cw_ref[:, col:col + width]
        b0 = HALO + r0
        return (w[0:1] * buf[slot, b0 - 1:b0 - 1 + FF_ROWS, 0:width] + w[1:2] * buf[slot, b0:b0 + FF_ROWS, 0:width]
                + w[2:3] * buf[slot, b0 + 1:b0 + 1 + FF_ROWS, 0:width] + w[3:4])

    starts = [sum(FF_CHUNKS[:c]) for c in range(len(FF_CHUNKS))]

    def up(c):
        c0, width, slot = starts[c], FF_CHUNKS[c], c % 2
        ug[slot, :, 0:width] = jnp.dot(hb, wup_ref[:, c0:c0 + width], preferred_element_type=F32)
        uv[slot, :, 0:width] = jnp.dot(hb, wup_ref[:, D_FF + c0:D_FF + c0 + width], preferred_element_type=F32)

    up(0)
    for c, (c0, width) in enumerate(zip(starts, FF_CHUNKS)):
        if c + 1 < len(FF_CHUNKS):
            up(c + 1)
        for r0 in range(0, TM, FF_ROWS):
            g = conv(ug, c % 2, r0, width, c0)
            v = conv(uv, c % 2, r0, width, D_FF + c0)
            act[c % 2, r0:r0 + FF_ROWS, 0:width] = (g * jax.nn.sigmoid(g) * v).astype(BF16)
        part = jnp.dot(act[c % 2, :, 0:width], wdn_ref[c0:c0 + width, :], preferred_element_type=F32)
        if c == 0:
            acc[...] = part
        else:
            acc[...] += part
    o_ref[...] = x + mod[5:6] * acc[...]


def _ffn(x2, mod, gn, wup, cw, wdn, seq):
    t = x2.shape[0]
    tiles_per_seq = seq // TM
    r = TM // HALO
    nh = t // HALO
    row = lambda i: (i, 0)
    prev = lambda i: (jnp.maximum(i * r - 1, 0), 0)
    nxt = lambda i: (jnp.minimum((i + 1) * r, nh - 1), 0)
    chunk = max(FF_CHUNKS)
    return pl.pallas_call(
        functools.partial(_ffn_kernel, tiles_per_seq=tiles_per_seq),
        grid=(t // TM,),
        in_specs=[
            pl.BlockSpec((HALO, D_MODEL), prev), pl.BlockSpec((TM, D_MODEL), row), pl.BlockSpec((HALO, D_MODEL), nxt),
            pl.BlockSpec((None, 8, D_MODEL), lambda i: (i // tiles_per_seq, 0, 0)),
            _const_spec((1, D_MODEL)),
            _const_spec((D_MODEL, 2 * D_FF)),
            _const_spec((8, 2 * D_FF)),
            _const_spec((D_FF, D_MODEL)),
        ],
        out_specs=pl.BlockSpec((TM, D_MODEL), row),
        out_shape=jax.ShapeDtypeStruct((t, D_MODEL), F32),
        scratch_shapes=[pltpu.VMEM((TM + 2 * HALO, D_MODEL), BF16),
                        pltpu.VMEM((2, TM + 2 * HALO, chunk), F32), pltpu.VMEM((2, TM + 2 * HALO, chunk), F32),
                        pltpu.VMEM((2, TM, chunk), BF16), pltpu.VMEM((TM, D_MODEL), F32)],
        compiler_params=_cparams(("parallel",)),
        name="ffn",
    )(x2, x2, x2, mod, gn, wup, cw, wdn)


def _swap_halves(a, head_dim):
    shp = a.shape
    a = a.reshape(shp[:-1] + (shp[-1] // head_dim, 2, head_dim // 2))
    return a[..., ::-1, :].reshape(shp)


def _rope_table(seq):
    pos = jnp.arange(seq, dtype=F32)[:, None]

    def one(dim):
        inv = 1.0 / (ROPE_THETA ** (jnp.arange(0, dim, 2, dtype=F32) / dim))
        ang = pos * inv[None, :]
        c, s = jnp.cos(ang), jnp.sin(ang)
        rep = 128 // dim
        return jnp.tile(jnp.concatenate([c, c], 1), (1, rep)), jnp.tile(jnp.concatenate([-s, s], 1), (1, rep))

    cc, sc = one(HEAD_C)
    cd, sd = one(HEAD_D)
    return jnp.concatenate([cc, sc, cd, sd], axis=1)


def _layer_params(l, w_in, conv_a, conv_b, conv_b_bias, ln_b_gain, ln_b_bias, g_q_c, g_k_c, sink_c, g_q_d, g_k_d,
                  g_subln_d, w_merge, w_branch, w_out, w_up, conv_ffn, conv_ffn_bias, w_down):
    w = w_in[l]
    w_ext = jnp.concatenate([
        w,
        _swap_halves(w[:, _CQ:_CK], HEAD_C), _swap_halves(w[:, _CK:_CV], HEAD_C),
        _swap_halves(w[:, _DQ:_DK], HEAD_D), _swap_halves(w[:, _DK:_DV], HEAD_D)], axis=1).astype(BF16)

    def tiled(g, scale):
        g = g.astype(F32) * scale
        rep = 256 // g.shape[0]
        return jnp.tile(g, rep), jnp.tile(_swap_halves(g, g.shape[0]), rep)

    rows = (tiled(g_q_c[l], HEAD_C ** -0.5 * LOG2E) + tiled(g_k_c[l], 1.0)
            + tiled(g_q_d[l], HEAD_D ** -0.5 * LOG2E) + tiled(g_k_d[l], 1.0))
    gains = jnp.stack(rows, axis=0)
    pad_rows = lambda a, n: jnp.pad(a.astype(F32), ((0, n - a.shape[0]), (0, 0)))
    return dict(
        w_ext=w_ext, gains=gains,
        conv_a=pad_rows(conv_a[l], 8), conv_b=pad_rows(conv_b[l], 32),
        bvec=pad_rows(jnp.stack([conv_b_bias[l], ln_b_gain[l], ln_b_bias[l]], 0), 8),
        sink=jnp.broadcast_to(pad_rows(sink_c[l][:, None] * LOG2E, 8), (8, 128)),
        gs=jnp.broadcast_to(g_subln_d[l].astype(F32)[:, None], (V_D, TQ_D)),
        wm=w_merge[l].astype(BF16), wb=w_branch[l].astype(BF16), wo=w_out[l].astype(BF16),
        wup=w_up[l].astype(BF16), wdn=w_down[l].astype(BF16),
        cw=pad_rows(jnp.concatenate([conv_ffn[l], conv_ffn_bias[l][None, :]], 0), 8),
    )


def _segment_ones():
    idx = jnp.arange(256)
    seg = lambda n: (idx[:, None] // n == idx[None, :] // n).astype(BF16)
    return jnp.stack([seg(HEAD_C), seg(HEAD_D)], 0)


def _trunk(x, mods, layers, seg, g_norm_mix, g_norm_ffn, lam_d):
    b, s, d = x.shape
    x2 = x.reshape(b * s, d)
    rope = _rope_table(s)
    for l in range(DEPTH):
        p = layers[l]
        gn1 = g_norm_mix[l].astype(F32)[None, :]
        gn2 = g_norm_ffn[l].astype(F32)[None, :]
        za, zb, qc, kc, vc, qd, kd, vd = _proj(x2, mods[l], gn1, p["w_ext"], rope, p["gains"], seg, s)
        ya, yb = _mix_ab(za, zb, p["conv_a"], p["conv_b"], p["bvec"], s)
        yc = _attn_c(qc, kc, vc, p["sink"], s)
        lam_init = 0.8 - 0.6 * math.exp(-0.3 * l)
        yd = _attn_d(qd.reshape(b, s, 256), kd.reshape(b, s, 256), vd.reshape(b, s, 256), lam_d[l].astype(F32),
                     p["gs"], lam_init).reshape(b * s, 256).astype(BF16)
        x2 = _merge(x2, mods[l], gn1, ya, yb, yc, yd, p["wm"], p["wb"], p["wo"], s)
        x2 = _ffn(x2, mods[l], gn2, p["wup"], p["cw"], p["wdn"], s)
    return x2.reshape(b, s, d)


def kernel(x_prompt, x_sample, c_prompt, c_sample, w_ada, b_ada, g_norm_mix, w_in, conv_a, conv_b, conv_b_bias,
           ln_b_gain, ln_b_bias, g_q_c, g_k_c, sink_c, g_q_d, g_k_d, lam_d, g_subln_d, w_merge, w_branch, w_out,
           g_norm_ffn, w_up, conv_ffn, conv_ffn_bias, w_down):
    bp, bs = x_prompt.shape[0], x_sample.shape[0]
    assert bp + bs <= 8
    c8 = jnp.concatenate([c_prompt, c_sample, jnp.zeros((8 - bp - bs, D_MODEL), F32)], axis=0)
    mod = _ada(c8, w_ada, b_ada)
    mod = mod.reshape(DEPTH, 8, N_ADA, D_MODEL)
    mod = jnp.pad(mod, ((0, 0), (0, 0), (0, 8 - N_ADA), (0, 0)))
    layers = [_layer_params(l, w_in, conv_a, conv_b, conv_b_bias, ln_b_gain, ln_b_bias, g_q_c, g_k_c, sink_c, g_q_d,
                            g_k_d, g_subln_d, w_merge, w_branch, w_out, w_up, conv_ffn, conv_ffn_bias, w_down)
              for l in range(DEPTH)]
    seg = _segment_ones()
    y_prompt = _trunk(x_prompt, [mod[l, :bp] for l in range(DEPTH)], layers, seg, g_norm_mix, g_norm_ffn, lam_d)
    y_sample = _trunk(x_sample, [mod[l, bp:bp + bs] for l in range(DEPTH)], layers, seg, g_norm_mix, g_norm_ffn,
                      lam_d)
    return (y_prompt, y_sample)
```

```python
import functools
import math

import jax
import jax.numpy as jnp
from jax import lax
from jax.experimental import pallas as pl
from jax.experimental.pallas import tpu as pltpu

F32 = jnp.float32
BF16 = jnp.bfloat16

D_MODEL = 1024
DEPTH = 2
BRANCH = 256
HEAD_C = 64
WINDOW = 128
HEAD_D = 32
V_D = 64
VA_D = 80
N_HEADS_D = 4
CONF_K = 31
D_FF = 2816
D_IN = 2560
N_ADA = 6
ROPE_THETA = 10000.0
EPS = 1e-6
NEG = -1e30
LOG2E = 1.4426950408889634

VMEM_LIMIT = 56 * 1024 * 1024

TM = 512
HALO = 16
TQ_C = 512
TQ_D = 256
TK_D = 256
UNROLL_D = 8
FF_CHUNKS = (512, 512, 512, 512, 512, 256)
FF_ROWS = 128


def _cparams(sem):
    return pltpu.CompilerParams(dimension_semantics=sem, vmem_limit_bytes=VMEM_LIMIT)


def _const_spec(shape):
    nd = len(shape)
    return pl.BlockSpec(shape, lambda *_: (0,) * nd, pipeline_mode=pl.Buffered(1))


def _modulated_norm(x, gain, shift, scale):
    ms = jnp.mean(x * x, axis=-1, keepdims=True)
    return x * lax.rsqrt(ms + EPS) * gain * (1.0 + scale) + shift


def _ada_kernel(c_ref, w_ref, b_ref, o_ref):
    c = c_ref[...]
    a = c * jax.nn.sigmoid(c)
    a_hi = a.astype(BF16)
    a_lo = (a - a_hi.astype(F32)).astype(BF16)
    w = w_ref[...]
    w_hi = w.astype(BF16)
    w_lo = (w - w_hi.astype(F32)).astype(BF16)
    acc = jnp.dot(a_hi, w_hi, preferred_element_type=F32)
    acc += jnp.dot(a_lo, w_hi, preferred_element_type=F32)
    acc += jnp.dot(a_hi, w_lo, preferred_element_type=F32)
    o_ref[...] = acc + b_ref[...]


def _ada(c8, w_ada, b_ada):
    tn = 1536
    n = N_ADA * D_MODEL
    return pl.pallas_call(
        _ada_kernel,
        grid=(DEPTH, n // tn),
        in_specs=[
            pl.BlockSpec((8, D_MODEL), lambda l, j: (0, 0)),
            pl.BlockSpec((None, D_MODEL, tn), lambda l, j: (l, 0, j)),
            pl.BlockSpec((None, 1, tn), lambda l, j: (l, 0, j)),
        ],
        out_specs=pl.BlockSpec((None, 8, tn), lambda l, j: (l, 0, j)),
        out_shape=jax.ShapeDtypeStruct((DEPTH, 8, n), F32),
        compiler_params=_cparams(("arbitrary", "arbitrary")),
        name="ada",
    )(c8, w_ada, b_ada.reshape(DEPTH, 1, n))


_A0, _B0, _CQ, _CK, _CV, _DQ, _DK, _DV = 0, 768, 1280, 1536, 1664, 1792, 2048, 2304
_CQS, _CKS, _DQS, _DKS, _WEXT = 2560, 2816, 2944, 3200, 3456


def _normed_rope(z, zsw, g, gsw, cos, sin, seg, inv_n):
    sq = z * z
    hi = sq.astype(BF16)
    lo = (sq - hi.astype(F32)).astype(BF16)
    ssq = jnp.dot(hi, seg, preferred_element_type=F32) + jnp.dot(lo, seg, preferred_element_type=F32)
    r = lax.rsqrt(ssq * inv_n + EPS)
    return r * (z * (g * cos) + zsw * (gsw * sin))


def _proj_kernel(x_ref, mod_ref, gn_ref, w_ref, rope_ref, gain_ref, seg_ref,
                 za_ref, zb_ref, qc_ref, kc_ref, vc_ref, qd_ref, kd_ref, vd_ref):
    mod = mod_ref[...]
    h = _modulated_norm(x_ref[...], gn_ref[...], mod[0:1], mod[1:2])
    hb = h.astype(BF16)

    def mm(c0, c1):
        return jnp.dot(hb, w_ref[:, c0:c1], preferred_element_type=F32)

    za_ref[...] = mm(_A0, _B0).astype(BF16)
    zb_ref[...] = mm(_B0, _CQ).astype(BF16)
    vc_ref[...] = mm(_CV, _DQ).astype(BF16)
    vd_ref[...] = mm(_DV, _CQS).astype(BF16)

    rope = rope_ref[...]
    cos_c, sin_c, cos_d, sin_d = (rope[:, 0:128], rope[:, 128:256], rope[:, 256:384], rope[:, 384:512])
    two = lambda t: jnp.concatenate([t, t], axis=1)
    gains = gain_ref[...]
    seg64 = seg_ref[0]
    seg32 = seg_ref[1]
    qc_ref[...] = _normed_rope(mm(_CQ, _CK), mm(_CQS, _CKS), gains[0:1], gains[1:2],
                               two(cos_c), two(sin_c), seg64, 1.0 / HEAD_C).astype(BF16)
    kc_ref[...] = _normed_rope(mm(_CK, _CV), mm(_CKS, _DQS), gains[2:3, 0:128], gains[3:4, 0:128],
                               cos_c, sin_c, seg64[0:128, 0:128], 1.0 / HEAD_C).astype(BF16)
    qd_ref[...] = _normed_rope(mm(_DQ, _DK), mm(_DQS, _DKS), gains[4:5], gains[5:6],
                               two(cos_d), two(sin_d), seg32, 1.0 / HEAD_D).astype(BF16)
    kd_ref[...] = _normed_rope(mm(_DK, _DV), mm(_DKS, _WEXT), gains[6:7], gains[7:8],
                               two(cos_d), two(sin_d), seg32, 1.0 / HEAD_D).astype(BF16)


def _proj(x2, mod, gn, w_ext, rope, gains, seg, seq):
    t = x2.shape[0]
    tiles_per_seq = seq // TM
    row = lambda i: (i, 0)
    widths = (768, 512, 256, 128, 128, 256, 256, 256)
    return pl.pallas_call(
        _proj_kernel,
        grid=(t // TM,),
        in_specs=[
            pl.BlockSpec((TM, D_MODEL), row),
            pl.BlockSpec((None, 8, D_MODEL), lambda i: (i // tiles_per_seq, 0, 0)),
            _const_spec((1, D_MODEL)),
            _const_spec((D_MODEL, _WEXT)),
            pl.BlockSpec((TM, 512), lambda i: (i % tiles_per_seq, 0)),
            _const_spec((8, 256)),
            _const_spec((2, 256, 256)),
        ],
        out_specs=[pl.BlockSpec((TM, w), row) for w in widths],
        out_shape=[jax.ShapeDtypeStruct((t, w), BF16) for w in widths],
        compiler_params=_cparams(("parallel",)),
        name="proj",
    )(x2, mod, gn, w_ext, rope, gains, seg)


def _dwconv_rows(load, weights, offsets, base, n_out):
    nt = n_out // 8
    t_lo = (base + min(offsets)) // 8
    t_hi = (base + max(offsets) + n_out - 1) // 8 + 1
    x = load(t_lo * 8, t_hi * 8)
    c = x.shape[1]
    x = x.reshape(t_hi - t_lo, 8, c)
    sub = lax.broadcasted_iota(jnp.int32, (nt, 8, c), 1)
    out = None
    for b in range(8):
        taps = [k for k, off in enumerate(offsets) if (base + off) % 8 == b]
        if not taps:
            continue
        xr = x if b == 0 else pltpu.roll(x, 8 - b, axis=1)
        ext = nt + (1 if b else 0)
        acc = None
        for k in taps:
            a = (base + offsets[k]) // 8 - t_lo
            term = weights[k] * xr[a:a + ext]
            acc = term if acc is None else acc + term
        part = acc if b == 0 else jnp.where(sub < 8 - b, acc[:-1], acc[1:])
        out = part if out is None else out + part
    return out.reshape(n_out, c)


def _mix_ab_kernel(zap_ref, za_ref, zan_ref, zbp_ref, zb_ref, zbn_ref, ca_ref, cb_ref, bvec_ref,
                   ya_ref, yb_ref, abuf, bbuf, *, tiles_per_seq):
    i = pl.program_id(0)
    pos = i % tiles_per_seq
    keep_prev = (pos != 0).astype(F32)
    keep_next = (pos != tiles_per_seq - 1).astype(F32)

    def a_in(z):
        z = z.astype(F32)
        return z[:, 256:512] * z[:, 512:768]

    def b_in(z):
        z = z.astype(F32)
        return z[:, 0:256] * jax.nn.sigmoid(z[:, 256:512])

    abuf[0:HALO, :] = a_in(zap_ref[...]) * keep_prev
    abuf[HALO:HALO + TM, :] = a_in(za_ref[...])
    abuf[HALO + TM:, :] = a_in(zan_ref[...]) * keep_next
    bbuf[0:HALO, :] = b_in(zbp_ref[...]) * keep_prev
    bbuf[HALO:HALO + TM, :] = b_in(zb_ref[...])
    bbuf[HALO + TM:, :] = b_in(zbn_ref[...]) * keep_next

    ca = ca_ref[...]
    conv_a = _dwconv_rows(lambda r0, r1: abuf[r0:r1, :], [ca[k:k + 1] for k in range(3)], (-1, 0, 1), HALO, TM)
    ya_ref[...] = (za_ref[:, 0:256].astype(F32) * conv_a).astype(BF16)

    cb = cb_ref[...]
    pad = (CONF_K - 1) // 2
    t = _dwconv_rows(lambda r0, r1: bbuf[r0:r1, :], [cb[k:k + 1] for k in range(CONF_K)],
                     tuple(range(-pad, pad + 1)), HALO, TM)
    bvec = bvec_ref[...]
    t = t + bvec[0:1]
    mu = jnp.mean(t, axis=-1, keepdims=True)
    tc = t - mu
    var = jnp.mean(tc * tc, axis=-1, keepdims=True)
    y = tc * lax.rsqrt(var + EPS) * bvec[1:2] + bvec[2:3]
    yb_ref[...] = (y * jax.nn.sigmoid(y)).astype(BF16)


def _mix_ab(za, zb, conv_a, conv_b, bvec, seq):
    t = za.shape[0]
    tiles_per_seq = seq // TM
    r = TM // HALO
    nh = t // HALO
    row = lambda i: (i, 0)
    prev = lambda i: (jnp.maximum(i * r - 1, 0), 0)
    nxt = lambda i: (jnp.minimum((i + 1) * r, nh - 1), 0)
    return pl.pallas_call(
        functools.partial(_mix_ab_kernel, tiles_per_seq=tiles_per_seq),
        grid=(t // TM,),
        in_specs=[
            pl.BlockSpec((HALO, 768), prev), pl.BlockSpec((TM, 768), row), pl.BlockSpec((HALO, 768), nxt),
            pl.BlockSpec((HALO, 512), prev), pl.BlockSpec((TM, 512), row), pl.BlockSpec((HALO, 512), nxt),
            _const_spec((8, 256)), _const_spec((32, 256)), _const_spec((8, 256)),
        ],
        out_specs=[pl.BlockSpec((TM, BRANCH), row), pl.BlockSpec((TM, BRANCH), row)],
        out_shape=[jax.ShapeDtypeStruct((t, BRANCH), BF16)] * 2,
        scratch_shapes=[pltpu.VMEM((TM + 2 * HALO, BRANCH), F32), pltpu.VMEM((TM + 2 * HALO, BRANCH), F32)],
        compiler_params=_cparams(("parallel",)),
        name="mix_ab",
    )(za, za, za, zb, zb, zb, conv_a, conv_b, bvec)


def _attn_c_kernel(q_ref, kp_ref, k_ref, kn_ref, vp_ref, v_ref, vn_ref, sink_ref, o_ref, s_ref, p_ref, ot_ref,
                   *, tiles_per_seq):
    i = pl.program_id(0)
    pos = i % tiles_per_seq
    first = pos == 0
    last = pos == tiles_per_seq - 1
    nsub = TQ_C // WINDOW

    kk = jnp.concatenate([kp_ref[...], k_ref[...], kn_ref[...]], axis=0)
    vt = jnp.concatenate([vp_ref[...], v_ref[...], vn_ref[...]], axis=0).astype(F32).T.astype(BF16)
    qt = q_ref[...].astype(F32).T
    zeros = jnp.zeros((HEAD_C, TQ_C), F32)
    sink = sink_ref[...]
    tt = lax.broadcasted_iota(jnp.int32, (3 * WINDOW, WINDOW), 0)
    aa = lax.broadcasted_iota(jnp.int32, (3 * WINDOW, WINDOW), 1)
    rel_ok = jnp.abs(tt - WINDOW - aa) <= WINDOW

    problems = [(j, h) for j in range(nsub) for h in range(4)]
    qz = []
    for h in range(4):
        qh = qt[h * HEAD_C:(h + 1) * HEAD_C, :]
        qz.append(jnp.concatenate([qh, zeros] if h // 2 == 0 else [zeros, qh], axis=0).astype(BF16))
    for n, (j, h) in enumerate(problems):
        r0 = j * WINDOW
        s_ref[n] = jnp.dot(kk[r0:r0 + 3 * WINDOW, :], qz[h][:, r0:r0 + WINDOW], preferred_element_type=F32)
    for n, (j, h) in enumerate(problems):
        ok = rel_ok
        if j == 0:
            ok = ok & (tt >= jnp.where(first, WINDOW, 0))
        if j == nsub - 1:
            ok = ok & (tt < jnp.where(last, 2 * WINDOW, 3 * WINDOW))
        s = jnp.where(ok, s_ref[n], NEG)
        sk = sink[h:h + 1, :]
        m = jnp.maximum(jnp.max(s, axis=0, keepdims=True), sk)
        p = jnp.exp2(s - m)
        den = jnp.sum(p, axis=0, keepdims=True) + jnp.exp2(sk - m)
        p_ref[n] = (p / den).astype(BF16)
    for n, (j, h) in enumerate(problems):
        r0 = j * WINDOW
        vh = vt[(h // 2) * HEAD_C:(h // 2 + 1) * HEAD_C, r0:r0 + 3 * WINDOW]
        ot_ref[h * HEAD_C:(h + 1) * HEAD_C, r0:r0 + WINDOW] = jnp.dot(vh, p_ref[n], preferred_element_type=F32)
    o_ref[...] = ot_ref[...].T.astype(BF16)


def _attn_c(qc, kc, vc, sink, seq):
    t = qc.shape[0]
    tiles_per_seq = seq // TQ_C
    r = TQ_C // WINDOW
    nh = t // WINDOW
    row = lambda i: (i, 0)
    prev = lambda i: (jnp.maximum(i * r - 1, 0), 0)
    nxt = lambda i: (jnp.minimum((i + 1) * r, nh - 1), 0)
    kv_specs = [pl.BlockSpec((WINDOW, 128), prev), pl.BlockSpec((TQ_C, 128), row), pl.BlockSpec((WINDOW, 128), nxt)]
    return pl.pallas_call(
        functools.partial(_attn_c_kernel, tiles_per_seq=tiles_per_seq),
        grid=(t // TQ_C,),
        in_specs=[pl.BlockSpec((TQ_C, BRANCH), row)] + kv_specs + kv_specs + [_const_spec((8, 128))],
        out_specs=pl.BlockSpec((TQ_C, BRANCH), row),
        out_shape=jax.ShapeDtypeStruct((t, BRANCH), BF16),
        scratch_shapes=[pltpu.VMEM((4 * r, 3 * WINDOW, WINDOW), F32), pltpu.VMEM((4 * r, 3 * WINDOW, WINDOW), BF16),
                        pltpu.VMEM((BRANCH, TQ_C), F32)],
        compiler_params=_cparams(("parallel",)),
        name="attn_c",
    )(qc, kc, kc, kc, vc, vc, vc, sink)


def _attn_d_kernel(lam_ref, gs_ref, qt_ref, k_ref, vt_ref, o_ref, s_ref, mx_ref, m_ref, acc_ref, *, nkt, lam_init):
    qt = qt_ref[...].astype(F32)
    grp = lax.broadcasted_iota(jnp.int32, (128, TQ_D), 0) // HEAD_D
    qz = []
    for g in range(8):
        half = qt[(g // 4) * 128:(g // 4 + 1) * 128, :]
        qz.append(jnp.where(grp == (g % 4), half, 0.0).astype(BF16))
    m_ref[...] = jnp.full(m_ref.shape, NEG, F32)
    acc_ref[...] = jnp.zeros(acc_ref.shape, F32)

    def scores(kk, slot, g):
        s = jnp.dot(kk[:, (g // 4) * 128:(g // 4 + 1) * 128], qz[g], preferred_element_type=F32)
        s_ref[slot, g] = s
        mx_ref[slot, g] = jnp.max(s.reshape(TK_D // 8, 8, TQ_D), axis=0)

    def consume(vt, slot, g):
        h = g // 2
        m_old = m_ref[g]
        m_new = jnp.maximum(m_old, jnp.max(mx_ref[slot, g], axis=0, keepdims=True))
        alpha = jnp.exp2(m_old - m_new)
        p = jnp.exp2(s_ref[slot, g] - m_new[0:1])
        pv = jnp.dot(vt[h * VA_D:(h + 1) * VA_D, :], p.astype(BF16), preferred_element_type=F32)
        acc_ref[g] = acc_ref[g] * alpha[0:1] + pv
        m_ref[g] = m_new

    kk0 = k_ref[0]
    for g in range(8):
        scores(kk0, 0, g)

    def body(u, carry):
        for j in range(UNROLL_D):
            kt = UNROLL_D * u + j
            kk = k_ref[jnp.minimum(kt + 1, nkt - 1)]
            vt = vt_ref[kt]
            for g in range(8):
                scores(kk, (j + 1) % 2, g)
                consume(vt, j % 2, g)
        return carry

    lax.fori_loop(0, nkt // UNROLL_D, body, 0)

    lf = lam_ref[...]
    lam = (jnp.exp(jnp.sum(lf[0:1] * lf[1:2], axis=-1, keepdims=True))
           - jnp.exp(jnp.sum(lf[2:3] * lf[3:4], axis=-1, keepdims=True)) + lam_init)
    gs = gs_ref[...]
    for h in range(N_HEADS_D):
        a1 = acc_ref[2 * h]
        a2 = acc_ref[2 * h + 1]
        o = a1[0:V_D] / a1[V_D:V_D + 1] - lam * (a2[0:V_D] / a2[V_D:V_D + 1])
        ms = jnp.mean(o * o, axis=0, keepdims=True)
        o_ref[h * V_D:(h + 1) * V_D, :] = o * lax.rsqrt(ms + EPS) * gs * (1.0 - lam_init)


def _attn_d(qd, kd, vd, lam_d, gs, lam_init):
    b, s, _ = qd.shape
    nkt = s // TK_D
    assert nkt % UNROLL_D == 0 and UNROLL_D % 2 == 0
    qt = qd.transpose(0, 2, 1)
    kt = kd.reshape(b, nkt, TK_D, 256)
    v4 = vd.reshape(b, nkt, TK_D, N_HEADS_D, V_D)
    ones = jnp.ones((b, nkt, TK_D, N_HEADS_D, 1), BF16)
    zeros = jnp.zeros((b, nkt, TK_D, N_HEADS_D, VA_D - V_D - 1), BF16)
    vt = jnp.concatenate([v4, ones, zeros], axis=-1).reshape(b, nkt, TK_D, N_HEADS_D * VA_D).transpose(0, 1, 3, 2)
    out = pl.pallas_call(
        functools.partial(_attn_d_kernel, nkt=nkt, lam_init=lam_init),
        grid=(b, s // TQ_D),
        in_specs=[
            _const_spec((4, HEAD_D)),
            _const_spec((V_D, TQ_D)),
            pl.BlockSpec((None, 256, TQ_D), lambda bi, qi: (bi, 0, qi)),
            pl.BlockSpec((None, nkt, TK_D, 256), lambda bi, qi: (bi, 0, 0, 0), pipeline_mode=pl.Buffered(1)),
            pl.BlockSpec((None, nkt, N_HEADS_D * VA_D, TK_D), lambda bi, qi: (bi, 0, 0, 0),
                         pipeline_mode=pl.Buffered(1)),
        ],
        out_specs=pl.BlockSpec((None, 256, TQ_D), lambda bi, qi: (bi, 0, qi)),
        out_shape=jax.ShapeDtypeStruct((b, 256, s), F32),
        scratch_shapes=[pltpu.VMEM((2, 8, TK_D, TQ_D), F32), pltpu.VMEM((2, 8, 8, TQ_D), F32),
                        pltpu.VMEM((8, 8, TQ_D), F32), pltpu.VMEM((8, VA_D, TQ_D), F32)],
        compiler_params=_cparams(("parallel", "arbitrary")),
        name="attn_d",
    )(lam_d, gs, qt, kt, vt)
    return out.transpose(0, 2, 1)


def _merge_kernel(x_ref, mod_ref, gn_ref, ya_ref, yb_ref, yc_ref, yd_ref, wm_ref, wb_ref, wo_ref, o_ref):
    x = x_ref[...]
    mod = mod_ref[...]
    hb = _modulated_norm(x, gn_ref[...], mod[0:1], mod[1:2]).astype(BF16)
    merged = None
    for i, y_ref in enumerate((ya_ref, yb_ref, yc_ref, yd_ref)):
        gate = jax.nn.sigmoid(jnp.dot(hb, wm_ref[i], preferred_element_type=F32))
        term = gate * jnp.dot(y_ref[...], wb_ref[i], preferred_element_type=F32)
        merged = term if merged is None else merged + term
    out = jnp.dot(merged.astype(BF16), wo_ref[...], preferred_element_type=F32)
    o_ref[...] = x + mod[2:3] * out


def _merge(x2, mod, gn, ya, yb, yc, yd, wm, wb, wo, seq):
    t = x2.shape[0]
    tiles_per_seq = seq // TM
    row = lambda i: (i, 0)
    return pl.pallas_call(
        _merge_kernel,
        grid=(t // TM,),
        in_specs=[
            pl.BlockSpec((TM, D_MODEL), row),
            pl.BlockSpec((None, 8, D_MODEL), lambda i: (i // tiles_per_seq, 0, 0)),
            _const_spec((1, D_MODEL)),
        ] + [pl.BlockSpec((TM, BRANCH), row)] * 4 + [
            _const_spec((4, D_MODEL, D_MODEL)),
            _const_spec((4, BRANCH, D_MODEL)),
            _const_spec((D_MODEL, D_MODEL)),
        ],
        out_specs=pl.BlockSpec((TM, D_MODEL), row),
        out_shape=jax.ShapeDtypeStruct((t, D_MODEL), F32),
        compiler_params=_cparams(("parallel",)),
        name="merge",
    )(x2, mod, gn, ya, yb, yc, yd, wm, wb, wo)


def _ffn_kernel(xp_ref, x_ref, xn_ref, mod_ref, gn_ref, wup_ref, cw_ref, wdn_ref, o_ref, hbuf, ug, uv, act, acc,
                *, tiles_per_seq):
    i = pl.program_id(0)
    pos = i % tiles_per_seq
    keep_prev = (pos != 0).astype(F32)
    keep_next = (pos != tiles_per_seq - 1).astype(F32)
    mod = mod_ref[...]
    gn = gn_ref[...]
    x = x_ref[...]
    norm = lambda v: _modulated_norm(v, gn, mod[3:4], mod[4:5])
    hbuf[0:HALO, :] = (norm(xp_ref[...]) * keep_prev).astype(BF16)
    hbuf[HALO:HALO + TM, :] = norm(x).astype(BF16)
    hbuf[HALO + TM:, :] = (norm(xn_ref[...]) * keep_next).astype(BF16)
    hb = hbuf[...]

    def conv(buf, slot, r0, width, col):
        w = cw_ref[:, col:col + width]
        return _dwconv_rows(lambda a, b: buf[slot, a:b, 0:width], [w[k:k + 1] for k in range(3)], (-1, 0, 1),
                            HALO + r0, FF_ROWS) + w[3:4]

    starts = [sum(FF_CHUNKS[:c]) for c in range(len(FF_CHUNKS))]

    def up(c):
        c0, width, slot = starts[c], FF_CHUNKS[c], c % 2
        ug[slot, :, 0:width] = jnp.dot(hb, wup_ref[:, c0:c0 + width], preferred_element_type=F32)
        uv[slot, :, 0:width] = jnp.dot(hb, wup_ref[:, D_FF + c0:D_FF + c0 + width], preferred_element_type=F32)

    up(0)
    for c, (c0, width) in enumerate(zip(starts, FF_CHUNKS)):
        if c + 1 < len(FF_CHUNKS):
            up(c + 1)
        for r0 in range(0, TM, FF_ROWS):
            g = conv(ug, c % 2, r0, width, c0)
            v = conv(uv, c % 2, r0, width, D_FF + c0)
            act[c % 2, r0:r0 + FF_ROWS, 0:width] = (g * jax.nn.sigmoid(g) * v).astype(BF16)
        part = jnp.dot(act[c % 2, :, 0:width], wdn_ref[c0:c0 + width, :], preferred_element_type=F32)
        if c == 0:
            acc[...] = part
        else:
            acc[...] += part
    o_ref[...] = x + mod[5:6] * acc[...]


def _ffn(x2, mod, gn, wup, cw, wdn, seq):
    t = x2.shape[0]
    tiles_per_seq = seq // TM
    r = TM // HALO
    nh = t // HALO
    row = lambda i: (i, 0)
    prev = lambda i: (jnp.maximum(i * r - 1, 0), 0)
    nxt = lambda i: (jnp.minimum((i + 1) * r, nh - 1), 0)
    chunk = max(FF_CHUNKS)
    return pl.pallas_call(
        functools.partial(_ffn_kernel, tiles_per_seq=tiles_per_seq),
        grid=(t // TM,),
        in_specs=[
            pl.BlockSpec((HALO, D_MODEL), prev), pl.BlockSpec((TM, D_MODEL), row), pl.BlockSpec((HALO, D_MODEL), nxt),
            pl.BlockSpec((None, 8, D_MODEL), lambda i: (i // tiles_per_seq, 0, 0)),
            _const_spec((1, D_MODEL)),
            _const_spec((D_MODEL, 2 * D_FF)),
            _const_spec((8, 2 * D_FF)),
            _const_spec((D_FF, D_MODEL)),
        ],
        out_specs=pl.BlockSpec((TM, D_MODEL), row),
        out_shape=jax.ShapeDtypeStruct((t, D_MODEL), F32),
        scratch_shapes=[pltpu.VMEM((TM + 2 * HALO, D_MODEL), BF16),
                        pltpu.VMEM((2, TM + 2 * HALO, chunk), F32), pltpu.VMEM((2, TM + 2 * HALO, chunk), F32),
                        pltpu.VMEM((2, TM, chunk), BF16), pltpu.VMEM((TM, D_MODEL), F32)],
        compiler_params=_cparams(("parallel",)),
        name="ffn",
    )(x2, x2, x2, mod, gn, wup, cw, wdn)


def _swap_halves(a, head_dim):
    shp = a.shape
    a = a.reshape(shp[:-1] + (shp[-1] // head_dim, 2, head_dim // 2))
    return a[..., ::-1, :].reshape(shp)


def _rope_table(seq):
    pos = jnp.arange(seq, dtype=F32)[:, None]

    def one(dim):
        inv = 1.0 / (ROPE_THETA ** (jnp.arange(0, dim, 2, dtype=F32) / dim))
        ang = pos * inv[None, :]
        c, s = jnp.cos(ang), jnp.sin(ang)
        rep = 128 // dim
        return jnp.tile(jnp.concatenate([c, c], 1), (1, rep)), jnp.tile(jnp.concatenate([-s, s], 1), (1, rep))

    cc, sc = one(HEAD_C)
    cd, sd = one(HEAD_D)
    return jnp.concatenate([cc, sc, cd, sd], axis=1)


def _layer_params(l, w_in, conv_a, conv_b, conv_b_bias, ln_b_gain, ln_b_bias, g_q_c, g_k_c, sink_c, g_q_d, g_k_d,
                  g_subln_d, w_merge, w_branch, w_out, w_up, conv_ffn, conv_ffn_bias, w_down):
    w = w_in[l]
    w_ext = jnp.concatenate([
        w,
        _swap_halves(w[:, _CQ:_CK], HEAD_C), _swap_halves(w[:, _CK:_CV], HEAD_C),
        _swap_halves(w[:, _DQ:_DK], HEAD_D), _swap_halves(w[:, _DK:_DV], HEAD_D)], axis=1).astype(BF16)

    def tiled(g, scale):
        g = g.astype(F32) * scale
        rep = 256 // g.shape[0]
        return jnp.tile(g, rep), jnp.tile(_swap_halves(g, g.shape[0]), rep)

    rows = (tiled(g_q_c[l], HEAD_C ** -0.5 * LOG2E) + tiled(g_k_c[l], 1.0)
            + tiled(g_q_d[l], HEAD_D ** -0.5 * LOG2E) + tiled(g_k_d[l], 1.0))
    gains = jnp.stack(rows, axis=0)
    pad_rows = lambda a, n: jnp.pad(a.astype(F32), ((0, n - a.shape[0]), (0, 0)))
    return dict(
        w_ext=w_ext, gains=gains,
        conv_a=pad_rows(conv_a[l], 8), conv_b=pad_rows(conv_b[l], 32),
        bvec=pad_rows(jnp.stack([conv_b_bias[l], ln_b_gain[l], ln_b_bias[l]], 0), 8),
        sink=jnp.broadcast_to(pad_rows(sink_c[l][:, None] * LOG2E, 8), (8, 128)),
        gs=jnp.broadcast_to(g_subln_d[l].astype(F32)[:, None], (V_D, TQ_D)),
        wm=w_merge[l].astype(BF16), wb=w_branch[l].astype(BF16), wo=w_out[l].astype(BF16),
        wup=w_up[l].astype(BF16), wdn=w_down[l].astype(BF16),
        cw=pad_rows(jnp.concatenate([conv_ffn[l], conv_ffn_bias[l][None, :]], 0), 8),
    )


def _segment_ones():
    idx = jnp.arange(256)
    seg = lambda n: (idx[:, None] // n == idx[None, :] // n).astype(BF16)
    return jnp.stack([seg(HEAD_C), seg(HEAD_D)], 0)


def _trunk(x, mods, layers, seg, g_norm_mix, g_norm_ffn, lam_d):
    b, s, d = x.shape
    x2 = x.reshape(b * s, d)
    rope = _rope_table(s)
    for l in range(DEPTH):
        p = layers[l]
        gn1 = g_norm_mix[l].astype(F32)[None, :]
        gn2 = g_norm_ffn[l].astype(F32)[None, :]
        za, zb, qc, kc, vc, qd, kd, vd = _proj(x2, mods[l], gn1, p["w_ext"], rope, p["gains"], seg, s)
        ya, yb = _mix_ab(za, zb, p["conv_a"], p["conv_b"], p["bvec"], s)
        yc = _attn_c(qc, kc, vc, p["sink"], s)
        lam_init = 0.8 - 0.6 * math.exp(-0.3 * l)
        yd = _attn_d(qd.reshape(b, s, 256), kd.reshape(b, s, 256), vd.reshape(b, s, 256), lam_d[l].astype(F32),
                     p["gs"], lam_init).reshape(b * s, 256).astype(BF16)
        x2 = _merge(x2, mods[l], gn1, ya, yb, yc, yd, p["wm"], p["wb"], p["wo"], s)
        x2 = _ffn(x2, mods[l], gn2, p["wup"], p["cw"], p["wdn"], s)
    return x2.reshape(b, s, d)


def kernel(x_prompt, x_sample, c_prompt, c_sample, w_ada, b_ada, g_norm_mix, w_in, conv_a, conv_b, conv_b_bias,
           ln_b_gain, ln_b_bias, g_q_c, g_k_c, sink_c, g_q_d, g_k_d, lam_d, g_subln_d, w_merge, w_branch, w_out,
           g_norm_ffn, w_up, conv_ffn, conv_ffn_bias, w_down):
    bp, bs = x_prompt.shape[0], x_sample.shape[0]
    assert bp + bs <= 8
    c8 = jnp.concatenate([c_prompt, c_sample, jnp.zeros((8 - bp - bs, D_MODEL), F32)], axis=0)
    mod = _ada(c8, w_ada, b_ada)
    mod = mod.reshape(DEPTH, 8, N_ADA, D_MODEL)
    mod = jnp.pad(mod, ((0, 0), (0, 0), (0, 8 - N_ADA), (0, 0)))
    layers = [_layer_params(l, w_in, conv_a, conv_b, conv_b_bias, ln_b_gain, ln_b_bias, g_q_c, g_k_c, sink_c, g_q_d,
                            g_k_d, g_subln_d, w_merge, w_branch, w_out, w_up, conv_ffn, conv_ffn_bias, w_down)
              for l in range(DEPTH)]
    seg = _segment_ones()
    y_prompt = _trunk(x_prompt, [mod[l, :bp] for l in range(DEPTH)], layers, seg, g_norm_mix, g_norm_ffn, lam_d)
    y_sample = _trunk(x_sample, [mod[l, bp:bp + bs] for l in range(DEPTH)], layers, seg, g_norm_mix, g_norm_ffn,
                      lam_d)
    return (y_prompt, y_sample)
```

```python
import functools
import math

import jax
import jax.numpy as jnp
from jax import lax
from jax.experimental import pallas as pl
from jax.experimental.pallas import tpu as pltpu

F32 = jnp.float32
BF16 = jnp.bfloat16

D_MODEL = 1024
DEPTH = 2
BRANCH = 256
HEAD_C = 64
WINDOW = 128
HEAD_D = 32
V_D = 64
VA_D = 80
N_HEADS_D = 4
CONF_K = 31
D_FF = 2816
D_IN = 2560
N_ADA = 6
ROPE_THETA = 10000.0
EPS = 1e-6
NEG = -1e30
LOG2E = 1.4426950408889634

VMEM_LIMIT = 56 * 1024 * 1024

TM = 512
HALO = 16
TQ_C = 512
TQ_D = 256
TK_D = 256
UNROLL_D = 8
FF_CHUNKS = (512, 512, 512, 512, 512, 256)
FF_ROWS = 128


def _cparams(sem):
    return pltpu.CompilerParams(dimension_semantics=sem, vmem_limit_bytes=VMEM_LIMIT)


def _const_spec(shape):
    nd = len(shape)
    return pl.BlockSpec(shape, lambda *_: (0,) * nd, pipeline_mode=pl.Buffered(1))


def _modulated_norm(x, gain, shift, scale):
    ms = jnp.mean(x * x, axis=-1, keepdims=True)
    return x * lax.rsqrt(ms + EPS) * gain * (1.0 + scale) + shift


def _ada_kernel(c_ref, w_ref, b_ref, o_ref):
    c = c_ref[...]
    a = c * jax.nn.sigmoid(c)
    a_hi = a.astype(BF16)
    a_lo = (a - a_hi.astype(F32)).astype(BF16)
    w = w_ref[...]
    w_hi = w.astype(BF16)
    w_lo = (w - w_hi.astype(F32)).astype(BF16)
    acc = jnp.dot(a_hi, w_hi, preferred_element_type=F32)
    acc += jnp.dot(a_lo, w_hi, preferred_element_type=F32)
    acc += jnp.dot(a_hi, w_lo, preferred_element_type=F32)
    o_ref[...] = acc + b_ref[...]


def _ada(c8, w_ada, b_ada):
    tn = 1536
    n = N_ADA * D_MODEL
    return pl.pallas_call(
        _ada_kernel,
        grid=(DEPTH, n // tn),
        in_specs=[
            pl.BlockSpec((8, D_MODEL), lambda l, j: (0, 0)),
            pl.BlockSpec((None, D_MODEL, tn), lambda l, j: (l, 0, j)),
            pl.BlockSpec((None, 1, tn), lambda l, j: (l, 0, j)),
        ],
        out_specs=pl.BlockSpec((None, 8, tn), lambda l, j: (l, 0, j)),
        out_shape=jax.ShapeDtypeStruct((DEPTH, 8, n), F32),
        compiler_params=_cparams(("arbitrary", "arbitrary")),
        name="ada",
    )(c8, w_ada, b_ada.reshape(DEPTH, 1, n))


_A0, _B0, _CQ, _CK, _CV, _DQ, _DK, _DV = 0, 768, 1280, 1536, 1664, 1792, 2048, 2304


def _swap_rotary_halves(z, half):
    lane = lax.broadcasted_iota(jnp.int32, (z.shape[0], 128), 1)
    first = (lane % (2 * half)) < half
    blocks = []
    for c in range(z.shape[1] // 128):
        blk = z[:, c * 128:(c + 1) * 128]
        up = pltpu.roll(blk, 128 - half, axis=1)
        down = pltpu.roll(blk, half, axis=1)
        blocks.append(jnp.where(first, up, down))
    return blocks[0] if len(blocks) == 1 else jnp.concatenate(blocks, axis=1)


def _normed_rope(z, half, g, gsw, cos, sin, seg):
    sq = z * z
    hi = sq.astype(BF16)
    lo = (sq - hi.astype(F32)).astype(BF16)
    ssq = jnp.dot(hi, seg, preferred_element_type=F32) + jnp.dot(lo, seg, preferred_element_type=F32)
    r = lax.rsqrt(ssq * (0.5 / half) + EPS)
    return r * (z * (g * cos) + _swap_rotary_halves(z, half) * (gsw * sin))


def _proj_kernel(x_ref, mod_ref, gn_ref, w_ref, rope_ref, gain_ref, seg_ref,
                 za_ref, zb_ref, qc_ref, kc_ref, vc_ref, qdt_ref, kd_ref, vdt_ref):
    mod = mod_ref[...]
    h = _modulated_norm(x_ref[...], gn_ref[...], mod[0:1], mod[1:2])
    hb = h.astype(BF16)

    def mm(c0, c1):
        return jnp.dot(hb, w_ref[:, c0:c1], preferred_element_type=F32)

    z_cq, z_ck, z_dq, z_dk = mm(_CQ, _CK), mm(_CK, _CV), mm(_DQ, _DK), mm(_DK, _DV)
    za_ref[...] = mm(_A0, _B0).astype(BF16)
    zb_ref[...] = mm(_B0, _CQ).astype(BF16)
    vc_ref[...] = mm(_CV, _DQ).astype(BF16)

    vt = mm(_DV, D_IN).astype(BF16).astype(F32).T
    ones_blk = (lax.broadcasted_iota(jnp.int32, (VA_D - V_D, TK_D), 0) == 0).astype(F32)
    for kt in range(TM // TK_D):
        rows = []
        for hd in range(N_HEADS_D):
            rows += [vt[hd * V_D:(hd + 1) * V_D, kt * TK_D:(kt + 1) * TK_D], ones_blk]
        vdt_ref[kt] = jnp.concatenate(rows, axis=0).astype(BF16)

    rope = rope_ref[...]
    cos_c, sin_c, cos_d, sin_d = (rope[:, 0:128], rope[:, 128:256], rope[:, 256:384], rope[:, 384:512])
    two = lambda t: jnp.concatenate([t, t], axis=1)
    gains = gain_ref[...]
    seg64 = seg_ref[0]
    seg32 = seg_ref[1]
    qc_ref[...] = _normed_rope(z_cq, HEAD_C // 2, gains[0:1], gains[1:2], two(cos_c), two(sin_c), seg64).astype(BF16)
    kc_ref[...] = _normed_rope(z_ck, HEAD_C // 2, gains[2:3, 0:128], gains[3:4, 0:128],
                               cos_c, sin_c, seg64[0:128, 0:128]).astype(BF16)
    qd = _normed_rope(z_dq, HEAD_D // 2, gains[4:5], gains[5:6], two(cos_d), two(sin_d), seg32)
    qdt_ref[...] = qd.T.astype(BF16)
    kd_ref[...] = _normed_rope(z_dk, HEAD_D // 2, gains[6:7], gains[7:8], two(cos_d), two(sin_d), seg32).astype(BF16)


def _proj(x2, mod, gn, w_in, rope, gains, seg, seq):
    t = x2.shape[0]
    tiles_per_seq = seq // TM
    row = lambda i: (i, 0)
    bf = lambda shape: jax.ShapeDtypeStruct(shape, BF16)
    widths = (768, 512, 256, 128, 128)
    return pl.pallas_call(
        _proj_kernel,
        grid=(t // TM,),
        in_specs=[
            pl.BlockSpec((TM, D_MODEL), row),
            pl.BlockSpec((None, 8, D_MODEL), lambda i: (i // tiles_per_seq, 0, 0)),
            _const_spec((1, D_MODEL)),
            _const_spec((D_MODEL, D_IN)),
            pl.BlockSpec((TM, 512), lambda i: (i % tiles_per_seq, 0)),
            _const_spec((8, 256)),
            _const_spec((2, 256, 256)),
        ],
        out_specs=[pl.BlockSpec((TM, w), row) for w in widths] + [
            pl.BlockSpec((256, TM), lambda i: (0, i)),
            pl.BlockSpec((TM, 256), row),
            pl.BlockSpec((TM // TK_D, N_HEADS_D * VA_D, TK_D), lambda i: (i, 0, 0)),
        ],
        out_shape=[bf((t, w)) for w in widths] + [
            bf((256, t)), bf((t, 256)), bf((t // TK_D, N_HEADS_D * VA_D, TK_D))],
        compiler_params=_cparams(("parallel",)),
        name="proj",
    )(x2, mod, gn, w_in, rope, gains, seg)


def _dwconv_rows(load, weights, offsets, base, n_out):
    nt = n_out // 8
    t_lo = (base + min(offsets)) // 8
    t_hi = (base + max(offsets) + n_out - 1) // 8 + 1
    x = load(t_lo * 8, t_hi * 8)
    c = x.shape[1]
    x = x.reshape(t_hi - t_lo, 8, c)
    sub = lax.broadcasted_iota(jnp.int32, (nt, 8, c), 1)
    out = None
    for b in range(8):
        taps = [k for k, off in enumerate(offsets) if (base + off) % 8 == b]
        if not taps:
            continue
        xr = x if b == 0 else pltpu.roll(x, 8 - b, axis=1)
        ext = nt + (1 if b else 0)
        acc = None
        for k in taps:
            a = (base + offsets[k]) // 8 - t_lo
            term = weights[k] * xr[a:a + ext]
            acc = term if acc is None else acc + term
        part = acc if b == 0 else jnp.where(sub < 8 - b, acc[:-1], acc[1:])
        out = part if out is None else out + part
    return out.reshape(n_out, c)


def _mix_ab_kernel(zap_ref, za_ref, zan_ref, zbp_ref, zb_ref, zbn_ref, ca_ref, cb_ref, bvec_ref,
                   ya_ref, yb_ref, abuf, bbuf, *, tiles_per_seq):
    i = pl.program_id(0)
    pos = i % tiles_per_seq
    keep_prev = (pos != 0).astype(F32)
    keep_next = (pos != tiles_per_seq - 1).astype(F32)

    def a_in(z):
        z = z.astype(F32)
        return z[:, 256:512] * z[:, 512:768]

    def b_in(z):
        z = z.astype(F32)
        return z[:, 0:256] * jax.nn.sigmoid(z[:, 256:512])

    abuf[0:HALO, :] = a_in(zap_ref[...]) * keep_prev
    abuf[HALO:HALO + TM, :] = a_in(za_ref[...])
    abuf[HALO + TM:, :] = a_in(zan_ref[...]) * keep_next
    bbuf[0:HALO, :] = b_in(zbp_ref[...]) * keep_prev
    bbuf[HALO:HALO + TM, :] = b_in(zb_ref[...])
    bbuf[HALO + TM:, :] = b_in(zbn_ref[...]) * keep_next

    ca = ca_ref[...]
    conv_a = _dwconv_rows(lambda r0, r1: abuf[r0:r1, :], [ca[k:k + 1] for k in range(3)], (-1, 0, 1), HALO, TM)
    ya_ref[...] = (za_ref[:, 0:256].astype(F32) * conv_a).astype(BF16)

    cb = cb_ref[...]
    pad = (CONF_K - 1) // 2
    t = _dwconv_rows(lambda r0, r1: bbuf[r0:r1, :], [cb[k:k + 1] for k in range(CONF_K)],
                     tuple(range(-pad, pad + 1)), HALO, TM)
    bvec = bvec_ref[...]
    t = t + bvec[0:1]
    mu = jnp.mean(t, axis=-1, keepdims=True)
    tc = t - mu
    var = jnp.mean(tc * tc, axis=-1, keepdims=True)
    y = tc * lax.rsqrt(var + EPS) * bvec[1:2] + bvec[2:3]
    yb_ref[...] = (y * jax.nn.sigmoid(y)).astype(BF16)


def _mix_ab(za, zb, conv_a, conv_b, bvec, seq):
    t = za.shape[0]
    tiles_per_seq = seq // TM
    r = TM // HALO
    nh = t // HALO
    row = lambda i: (i, 0)
    prev = lambda i: (jnp.maximum(i * r - 1, 0), 0)
    nxt = lambda i: (jnp.minimum((i + 1) * r, nh - 1), 0)
    return pl.pallas_call(
        functools.partial(_mix_ab_kernel, tiles_per_seq=tiles_per_seq),
        grid=(t // TM,),
        in_specs=[
            pl.BlockSpec((HALO, 768), prev), pl.BlockSpec((TM, 768), row), pl.BlockSpec((HALO, 768), nxt),
            pl.BlockSpec((HALO, 512), prev), pl.BlockSpec((TM, 512), row), pl.BlockSpec((HALO, 512), nxt),
            _const_spec((8, 256)), _const_spec((32, 256)), _const_spec((8, 256)),
        ],
        out_specs=[pl.BlockSpec((TM, BRANCH), row), pl.BlockSpec((TM, BRANCH), row)],
        out_shape=[jax.ShapeDtypeStruct((t, BRANCH), BF16)] * 2,
        scratch_shapes=[pltpu.VMEM((TM + 2 * HALO, BRANCH), F32), pltpu.VMEM((TM + 2 * HALO, BRANCH), F32)],
        compiler_params=_cparams(("parallel",)),
        name="mix_ab",
    )(za, za, za, zb, zb, zb, conv_a, conv_b, bvec)


def _attn_c_kernel(q_ref, kp_ref, k_ref, kn_ref, vp_ref, v_ref, vn_ref, sink_ref, o_ref, s_ref, p_ref, ot_ref,
                   *, tiles_per_seq):
    i = pl.program_id(0)
    pos = i % tiles_per_seq
    first = pos == 0
    last = pos == tiles_per_seq - 1
    nsub = TQ_C // WINDOW

    kk = jnp.concatenate([kp_ref[...], k_ref[...], kn_ref[...]], axis=0)
    vt = jnp.concatenate([vp_ref[...], v_ref[...], vn_ref[...]], axis=0).astype(F32).T.astype(BF16)
    qt = q_ref[...].astype(F32).T
    zeros = jnp.zeros((HEAD_C, TQ_C), F32)
    sink = sink_ref[...]
    tt = lax.broadcasted_iota(jnp.int32, (3 * WINDOW, WINDOW), 0)
    aa = lax.broadcasted_iota(jnp.int32, (3 * WINDOW, WINDOW), 1)
    rel_ok = jnp.abs(tt - WINDOW - aa) <= WINDOW

    problems = [(j, h) for j in range(nsub) for h in range(4)]
    qz = []
    for h in range(4):
        qh = qt[h * HEAD_C:(h + 1) * HEAD_C, :]
        qz.append(jnp.concatenate([qh, zeros] if h // 2 == 0 else [zeros, qh], axis=0).astype(BF16))
    for n, (j, h) in enumerate(problems):
        r0 = j * WINDOW
        s_ref[n] = jnp.dot(kk[r0:r0 + 3 * WINDOW, :], qz[h][:, r0:r0 + WINDOW], preferred_element_type=F32)
    for n, (j, h) in enumerate(problems):
        ok = rel_ok
        if j == 0:
            ok = ok & (tt >= jnp.where(first, WINDOW, 0))
        if j == nsub - 1:
            ok = ok & (tt < jnp.where(last, 2 * WINDOW, 3 * WINDOW))
        s = jnp.where(ok, s_ref[n], NEG)
        sk = sink[h:h + 1, :]
        m = jnp.maximum(jnp.max(s, axis=0, keepdims=True), sk)
        p = jnp.exp2(s - m)
        den = jnp.sum(p, axis=0, keepdims=True) + jnp.exp2(sk - m)
        p_ref[n] = (p / den).astype(BF16)
    for n, (j, h) in enumerate(problems):
        r0 = j * WINDOW
        vh = vt[(h // 2) * HEAD_C:(h // 2 + 1) * HEAD_C, r0:r0 + 3 * WINDOW]
        ot_ref[h * HEAD_C:(h + 1) * HEAD_C, r0:r0 + WINDOW] = jnp.dot(vh, p_ref[n], preferred_element_type=F32)
    o_ref[...] = ot_ref[...].T.astype(BF16)


def _attn_c(qc, kc, vc, sink, seq):
    t = qc.shape[0]
    tiles_per_seq = seq // TQ_C
    r = TQ_C // WINDOW
    nh = t // WINDOW
    row = lambda i: (i, 0)
    prev = lambda i: (jnp.maximum(i * r - 1, 0), 0)
    nxt = lambda i: (jnp.minimum((i + 1) * r, nh - 1), 0)
    kv_specs = [pl.BlockSpec((WINDOW, 128), prev), pl.BlockSpec((TQ_C, 128), row), pl.BlockSpec((WINDOW, 128), nxt)]
    return pl.pallas_call(
        functools.partial(_attn_c_kernel, tiles_per_seq=tiles_per_seq),
        grid=(t // TQ_C,),
        in_specs=[pl.BlockSpec((TQ_C, BRANCH), row)] + kv_specs + kv_specs + [_const_spec((8, 128))],
        out_specs=pl.BlockSpec((TQ_C, BRANCH), row),
        out_shape=jax.ShapeDtypeStruct((t, BRANCH), BF16),
        scratch_shapes=[pltpu.VMEM((4 * r, 3 * WINDOW, WINDOW), F32), pltpu.VMEM((4 * r, 3 * WINDOW, WINDOW), BF16),
                        pltpu.VMEM((BRANCH, TQ_C), F32)],
        compiler_params=_cparams(("parallel",)),
        name="attn_c",
    )(qc, kc, kc, kc, vc, vc, vc, sink)


def _attn_d_kernel(lam_ref, gs_ref, qt_ref, k_ref, vt_ref, o_ref, s_ref, mx_ref, m_ref, acc_ref, ot_ref,
                   *, nkt, lam_init):
    qt = qt_ref[...].astype(F32)
    grp = lax.broadcasted_iota(jnp.int32, (128, TQ_D), 0) // HEAD_D
    qz = []
    for g in range(8):
        half = qt[(g // 4) * 128:(g // 4 + 1) * 128, :]
        qz.append(jnp.where(grp == (g % 4), half, 0.0).astype(BF16))
    m_ref[...] = jnp.full(m_ref.shape, NEG, F32)
    acc_ref[...] = jnp.zeros(acc_ref.shape, F32)

    def scores(kk, slot, g):
        s = jnp.dot(kk[:, (g // 4) * 128:(g // 4 + 1) * 128], qz[g], preferred_element_type=F32)
        s_ref[slot, g] = s
        mx_ref[slot, g] = jnp.max(s.reshape(TK_D // 8, 8, TQ_D), axis=0)

    def consume(vt, slot, g):
        h = g // 2
        m_old = m_ref[g]
        m_new = jnp.maximum(m_old, jnp.max(mx_ref[slot, g], axis=0, keepdims=True))
        alpha = jnp.exp2(m_old - m_new)
        p = jnp.exp2(s_ref[slot, g] - m_new[0:1])
        pv = jnp.dot(vt[h * VA_D:(h + 1) * VA_D, :], p.astype(BF16), preferred_element_type=F32)
        acc_ref[g] = acc_ref[g] * alpha[0:1] + pv
        m_ref[g] = m_new

    kk0 = k_ref[0]
    for g in range(8):
        scores(kk0, 0, g)

    def body(u, carry):
        for j in range(UNROLL_D):
            kt = UNROLL_D * u + j
            kk = k_ref[jnp.minimum(kt + 1, nkt - 1)]
            vt = vt_ref[kt]
            for g in range(8):
                scores(kk, (j + 1) % 2, g)
                consume(vt, j % 2, g)
        return carry

    lax.fori_loop(0, nkt // UNROLL_D, body, 0)

    lf = lam_ref[...]
    lam = (jnp.exp(jnp.sum(lf[0:1] * lf[1:2], axis=-1, keepdims=True))
           - jnp.exp(jnp.sum(lf[2:3] * lf[3:4], axis=-1, keepdims=True)) + lam_init)
    gs = gs_ref[...]
    for h in range(N_HEADS_D):
        a1 = acc_ref[2 * h]
        a2 = acc_ref[2 * h + 1]
        o = a1[0:V_D] / a1[V_D:V_D + 1] - lam * (a2[0:V_D] / a2[V_D:V_D + 1])
        ms = jnp.mean(o * o, axis=0, keepdims=True)
        ot_ref[h * V_D:(h + 1) * V_D, :] = o * lax.rsqrt(ms + EPS) * gs * (1.0 - lam_init)
    o_ref[...] = ot_ref[...].T.astype(BF16)


def _attn_d(qdt, kd, vdt, lam_d, gs, lam_init, b, s):
    nkt = s // TK_D
    nqt = s // TQ_D
    assert nkt % UNROLL_D == 0 and UNROLL_D % 2 == 0
    return pl.pallas_call(
        functools.partial(_attn_d_kernel, nkt=nkt, lam_init=lam_init),
        grid=(b, nqt),
        in_specs=[
            _const_spec((4, HEAD_D)),
            _const_spec((V_D, TQ_D)),
            pl.BlockSpec((256, TQ_D), lambda bi, qi: (0, bi * nqt + qi)),
            pl.BlockSpec((None, nkt, TK_D, 256), lambda bi, qi: (bi, 0, 0, 0), pipeline_mode=pl.Buffered(1)),
            pl.BlockSpec((None, nkt, N_HEADS_D * VA_D, TK_D), lambda bi, qi: (bi, 0, 0, 0),
                         pipeline_mode=pl.Buffered(1)),
        ],
        out_specs=pl.BlockSpec((TQ_D, 256), lambda bi, qi: (bi * nqt + qi, 0)),
        out_shape=jax.ShapeDtypeStruct((b * s, 256), BF16),
        scratch_shapes=[pltpu.VMEM((2, 8, TK_D, TQ_D), F32), pltpu.VMEM((2, 8, 8, TQ_D), F32),
                        pltpu.VMEM((8, 8, TQ_D), F32), pltpu.VMEM((8, VA_D, TQ_D), F32),
                        pltpu.VMEM((N_HEADS_D * V_D, TQ_D), F32)],
        compiler_params=_cparams(("parallel", "arbitrary")),
        name="attn_d",
    )(lam_d, gs, qdt, kd.reshape(b, nkt, TK_D, 256), vdt.reshape(b, nkt, N_HEADS_D * VA_D, TK_D))


def _merge_kernel(x_ref, mod_ref, gn_ref, ya_ref, yb_ref, yc_ref, yd_ref, wm_ref, wb_ref, wo_ref, o_ref):
    mod = mod_ref[...]
    gn = gn_ref[...]
    half = TM // 2

    def gated(r0):
        hb = _modulated_norm(x_ref[r0:r0 + half, :], gn, mod[0:1], mod[1:2]).astype(BF16)
        merged = None
        for i, y_ref in enumerate((ya_ref, yb_ref, yc_ref, yd_ref)):
            gate = jax.nn.sigmoid(jnp.dot(hb, wm_ref[i], preferred_element_type=F32))
            term = gate * jnp.dot(y_ref[r0:r0 + half, :], wb_ref[i], preferred_element_type=F32)
            merged = term if merged is None else merged + term
        return merged.astype(BF16)

    halves = [gated(0), gated(half)]
    for n, r0 in enumerate((0, half)):
        out = jnp.dot(halves[n], wo_ref[...], preferred_element_type=F32)
        o_ref[r0:r0 + half, :] = x_ref[r0:r0 + half, :] + mod[2:3] * out


def _merge(x2, mod, gn, ya, yb, yc, yd, wm, wb, wo, seq):
    t = x2.shape[0]
    tiles_per_seq = seq // TM
    row = lambda i: (i, 0)
    return pl.pallas_call(
        _merge_kernel,
        grid=(t // TM,),
        in_specs=[
            pl.BlockSpec((TM, D_MODEL), row),
            pl.BlockSpec((None, 8, D_MODEL), lambda i: (i // tiles_per_seq, 0, 0)),
            _const_spec((1, D_MODEL)),
        ] + [pl.BlockSpec((TM, BRANCH), row)] * 4 + [
            _const_spec((4, D_MODEL, D_MODEL)),
            _const_spec((4, BRANCH, D_MODEL)),
            _const_spec((D_MODEL, D_MODEL)),
        ],
        out_specs=pl.BlockSpec((TM, D_MODEL), row),
        out_shape=jax.ShapeDtypeStruct((t, D_MODEL), F32),
        compiler_params=_cparams(("parallel",)),
        name="merge",
    )(x2, mod, gn, ya, yb, yc, yd, wm, wb, wo)


def _ffn_kernel(xp_ref, x_ref, xn_ref, mod_ref, gn_ref, wup_ref, cw_ref, wdn_ref, o_ref, hbuf, ug, uv, act, acc,
                *, tiles_per_seq):
    i = pl.program_id(0)
    pos = i % tiles_per_seq
    keep_prev = (pos != 0).astype(F32)
    keep_next = (pos != tiles_per_seq - 1).astype(F32)
    mod = mod_ref[...]
    gn = gn_ref[...]
    x = x_ref[...]
    norm = lambda v: _modulated_norm(v, gn, mod[3:4], mod[4:5])
    hbuf[0:HALO, :] = (norm(xp_ref[...]) * keep_prev).astype(BF16)
    hbuf[HALO:HALO + TM, :] = norm(x).astype(BF16)
    hbuf[HALO + TM:, :] = (norm(xn_ref[...]) * keep_next).astype(BF16)
    hb = hbuf[...]

    def conv(buf, slot, r0, width, col):
        w = cw_ref[:, col:col + width]
        return _dwconv_rows(lambda a, b: buf[slot, a:b, 0:width], [w[k:k + 1] for k in range(3)], (-1, 0, 1),
                            HALO + r0, FF_ROWS) + w[3:4]

    starts = [sum(FF_CHUNKS[:c]) for c in range(len(FF_CHUNKS))]

    def up(c):
        c0, width, slot = starts[c], FF_CHUNKS[c], c % 2
        ug[slot, :, 0:width] = jnp.dot(hb, wup_ref[:, c0:c0 + width], preferred_element_type=F32)
        uv[slot, :, 0:width] = jnp.dot(hb, wup_ref[:, D_FF + c0:D_FF + c0 + width], preferred_element_type=F32)

    up(0)
    for c, (c0, width) in enumerate(zip(starts, FF_CHUNKS)):
        if c + 1 < len(FF_CHUNKS):
            up(c + 1)
        for r0 in range(0, TM, FF_ROWS):
            g = conv(ug, c % 2, r0, width, c0)
            v = conv(uv, c % 2, r0, width, D_FF + c0)
            act[c % 2, r0:r0 + FF_ROWS, 0:width] = (g * jax.nn.sigmoid(g) * v).astype(BF16)
        part = jnp.dot(act[c % 2, :, 0:width], wdn_ref[c0:c0 + width, :], preferred_element_type=F32)
        if c == 0:
            acc[...] = part
        else:
            acc[...] += part
    o_ref[...] = x + mod[5:6] * acc[...]


def _ffn(x2, mod, gn, wup, cw, wdn, seq):
    t = x2.shape[0]
    tiles_per_seq = seq // TM
    r = TM // HALO
    nh = t // HALO
    row = lambda i: (i, 0)
    prev = lambda i: (jnp.maximum(i * r - 1, 0), 0)
    nxt = lambda i: (jnp.minimum((i + 1) * r, nh - 1), 0)
    chunk = max(FF_CHUNKS)
    return pl.pallas_call(
        functools.partial(_ffn_kernel, tiles_per_seq=tiles_per_seq),
        grid=(t // TM,),
        in_specs=[
            pl.BlockSpec((HALO, D_MODEL), prev), pl.BlockSpec((TM, D_MODEL), row), pl.BlockSpec((HALO, D_MODEL), nxt),
            pl.BlockSpec((None, 8, D_MODEL), lambda i: (i // tiles_per_seq, 0, 0)),
            _const_spec((1, D_MODEL)),
            _const_spec((D_MODEL, 2 * D_FF)),
            _const_spec((8, 2 * D_FF)),
            _const_spec((D_FF, D_MODEL)),
        ],
        out_specs=pl.BlockSpec((TM, D_MODEL), row),
        out_shape=jax.ShapeDtypeStruct((t, D_MODEL), F32),
        scratch_shapes=[pltpu.VMEM((TM + 2 * HALO, D_MODEL), BF16),
                        pltpu.VMEM((2, TM + 2 * HALO, chunk), F32), pltpu.VMEM((2, TM + 2 * HALO, chunk), F32),
                        pltpu.VMEM((2, TM, chunk), BF16), pltpu.VMEM((TM, D_MODEL), F32)],
        compiler_params=_cparams(("parallel",)),
        name="ffn",
    )(x2, x2, x2, mod, gn, wup, cw, wdn)


def _swap_halves(a, head_dim):
    shp = a.shape
    a = a.reshape(shp[:-1] + (shp[-1] // head_dim, 2, head_dim // 2))
    return a[..., ::-1, :].reshape(shp)


def _rope_table(seq):
    pos = jnp.arange(seq, dtype=F32)[:, None]

    def one(dim):
        inv = 1.0 / (ROPE_THETA ** (jnp.arange(0, dim, 2, dtype=F32) / dim))
        ang = pos * inv[None, :]
        c, s = jnp.cos(ang), jnp.sin(ang)
        rep = 128 // dim
        return jnp.tile(jnp.concatenate([c, c], 1), (1, rep)), jnp.tile(jnp.concatenate([-s, s], 1), (1, rep))

    cc, sc = one(HEAD_C)
    cd, sd = one(HEAD_D)
    return jnp.concatenate([cc, sc, cd, sd], axis=1)


def _layer_params(l, w_in, conv_a, conv_b, conv_b_bias, ln_b_gain, ln_b_bias, g_q_c, g_k_c, sink_c, g_q_d, g_k_d,
                  g_subln_d, w_merge, w_branch, w_out, w_up, conv_ffn, conv_ffn_bias, w_down):
    def tiled(g, scale):
        g = g.astype(F32) * scale
        rep = 256 // g.shape[0]
        return jnp.tile(g, rep), jnp.tile(_swap_halves(g, g.shape[0]), rep)

    rows = (tiled(g_q_c[l], HEAD_C ** -0.5 * LOG2E) + tiled(g_k_c[l], 1.0)
            + tiled(g_q_d[l], HEAD_D ** -0.5 * LOG2E) + tiled(g_k_d[l], 1.0))
    gains = jnp.stack(rows, axis=0)
    pad_rows = lambda a, n: jnp.pad(a.astype(F32), ((0, n - a.shape[0]), (0, 0)))
    return dict(
        w_in=w_in[l].astype(BF16), gains=gains,
        conv_a=pad_rows(conv_a[l], 8), conv_b=pad_rows(conv_b[l], 32),
        bvec=pad_rows(jnp.stack([conv_b_bias[l], ln_b_gain[l], ln_b_bias[l]], 0), 8),
        sink=jnp.broadcast_to(pad_rows(sink_c[l][:, None] * LOG2E, 8), (8, 128)),
        gs=jnp.broadcast_to(g_subln_d[l].astype(F32)[:, None], (V_D, TQ_D)),
        wm=w_merge[l].astype(BF16), wb=w_branch[l].astype(BF16), wo=w_out[l].astype(BF16),
        wup=w_up[l].astype(BF16), wdn=w_down[l].astype(BF16),
        cw=pad_rows(jnp.concatenate([conv_ffn[l], conv_ffn_bias[l][None, :]], 0), 8),
    )


def _segment_ones():
    idx = jnp.arange(256)
    seg = lambda n: (idx[:, None] // n == idx[None, :] // n).astype(BF16)
    return jnp.stack([seg(HEAD_C), seg(HEAD_D)], 0)


def _trunk(x, mods, layers, seg, g_norm_mix, g_norm_ffn, lam_d):
    b, s, d = x.shape
    x2 = x.reshape(b * s, d)
    rope = _rope_table(s)
    for l in range(DEPTH):
        p = layers[l]
        gn1 = g_norm_mix[l].astype(F32)[None, :]
        gn2 = g_norm_ffn[l].astype(F32)[None, :]
        za, zb, qc, kc, vc, qdt, kd, vdt = _proj(x2, mods[l], gn1, p["w_in"], rope, p["gains"], seg, s)
        ya, yb = _mix_ab(za, zb, p["conv_a"], p["conv_b"], p["bvec"], s)
        yc = _attn_c(qc, kc, vc, p["sink"], s)
        lam_init = 0.8 - 0.6 * math.exp(-0.3 * l)
        yd = _attn_d(qdt, kd, vdt, lam_d[l].astype(F32), p["gs"], lam_init, b, s)
        x2 = _merge(x2, mods[l], gn1, ya, yb, yc, yd, p["wm"], p["wb"], p["wo"], s)
        x2 = _ffn(x2, mods[l], gn2, p["wup"], p["cw"], p["wdn"], s)
    return x2.reshape(b, s, d)


def kernel(x_prompt, x_sample, c_prompt, c_sample, w_ada, b_ada, g_norm_mix, w_in, conv_a, conv_b, conv_b_bias,
           ln_b_gain, ln_b_bias, g_q_c, g_k_c, sink_c, g_q_d, g_k_d, lam_d, g_subln_d, w_merge, w_branch, w_out,
           g_norm_ffn, w_up, conv_ffn, conv_ffn_bias, w_down):
    bp, bs = x_prompt.shape[0], x_sample.shape[0]
    assert bp + bs <= 8
    c8 = jnp.concatenate([c_prompt, c_sample, jnp.zeros((8 - bp - bs, D_MODEL), F32)], axis=0)
    mod = _ada(c8, w_ada, b_ada)
    mod = mod.reshape(DEPTH, 8, N_ADA, D_MODEL)
    mod = jnp.pad(mod, ((0, 0), (0, 0), (0, 8 - N_ADA), (0, 0)))
    layers = [_layer_params(l, w_in, conv_a, conv_b, conv_b_bias, ln_b_gain, ln_b_bias, g_q_c, g_k_c, sink_c, g_q_d,
                            g_k_d, g_subln_d, w_merge, w_branch, w_out, w_up, conv_ffn, conv_ffn_bias, w_down)
              for l in range(DEPTH)]
    seg = _segment_ones()
    y_prompt = _trunk(x_prompt, [mod[l, :bp] for l in range(DEPTH)], layers, seg, g_norm_mix, g_norm_ffn, lam_d)
    y_sample = _trunk(x_sample, [mod[l, bp:bp + bs] for l in range(DEPTH)], layers, seg, g_norm_mix, g_norm_ffn,
                      lam_d)
    return (y_prompt, y_sample)
```

```python
import functools
import math

import jax
import jax.numpy as jnp
from jax import lax
from jax.experimental import pallas as pl
from jax.experimental.pallas import tpu as pltpu

F32 = jnp.float32
BF16 = jnp.bfloat16
F8 = jnp.float8_e4m3fn

D_MODEL = 1024
DEPTH = 2
BRANCH = 256
HEAD_C = 64
WINDOW = 128
HEAD_D = 32
V_D = 64
VA_D = 80
N_HEADS_D = 4
F8_Q_SCALE = 2.0
F8_MAX_GAIN = 100.0
CONF_K = 31
D_FF = 2816
D_IN = 2560
N_ADA = 6
ROPE_THETA = 10000.0
EPS = 1e-6
NEG = -1e30
LOG2E = 1.4426950408889634

VMEM_LIMIT = 56 * 1024 * 1024

TM = 512
HALO = 16
TQ_C = 512
TQ_D = 256
TK_D = 256
UNROLL_D = 8
FF_CHUNKS = (512, 512, 512, 512, 512, 256)
FF_ROWS = 128


def _cparams(sem):
    return pltpu.CompilerParams(dimension_semantics=sem, vmem_limit_bytes=VMEM_LIMIT)


def _const_spec(shape):
    nd = len(shape)
    return pl.BlockSpec(shape, lambda *_: (0,) * nd, pipeline_mode=pl.Buffered(1))


def _modulated_norm(x, gain, shift, scale):
    ms = jnp.mean(x * x, axis=-1, keepdims=True)
    return x * lax.rsqrt(ms + EPS) * gain * (1.0 + scale) + shift


def _ada_kernel(c_ref, w_ref, b_ref, o_ref):
    c = c_ref[...]
    a = c * jax.nn.sigmoid(c)
    a_hi = a.astype(BF16)
    a_lo = (a - a_hi.astype(F32)).astype(BF16)
    w = w_ref[...]
    w_hi = w.astype(BF16)
    w_lo = (w - w_hi.astype(F32)).astype(BF16)
    acc = jnp.dot(a_hi, w_hi, preferred_element_type=F32)
    acc += jnp.dot(a_lo, w_hi, preferred_element_type=F32)
    acc += jnp.dot(a_hi, w_lo, preferred_element_type=F32)
    o_ref[...] = acc + b_ref[...]


def _ada(c8, w_ada, b_ada):
    tn = 1536
    n = N_ADA * D_MODEL
    return pl.pallas_call(
        _ada_kernel,
        grid=(DEPTH, n // tn),
        in_specs=[
            pl.BlockSpec((8, D_MODEL), lambda l, j: (0, 0)),
            pl.BlockSpec((None, D_MODEL, tn), lambda l, j: (l, 0, j)),
            pl.BlockSpec((None, 1, tn), lambda l, j: (l, 0, j)),
        ],
        out_specs=pl.BlockSpec((None, 8, tn), lambda l, j: (l, 0, j)),
        out_shape=jax.ShapeDtypeStruct((DEPTH, 8, n), F32),
        compiler_params=_cparams(("arbitrary", "arbitrary")),
        name="ada",
    )(c8, w_ada, b_ada.reshape(DEPTH, 1, n))


_A0, _B0, _CQ, _CK, _CV, _DQ, _DK, _DV = 0, 768, 1280, 1536, 1664, 1792, 2048, 2304


def _swap_rotary_halves(z, half):
    lane = lax.broadcasted_iota(jnp.int32, (z.shape[0], 128), 1)
    first = (lane % (2 * half)) < half
    blocks = []
    for c in range(z.shape[1] // 128):
        blk = z[:, c * 128:(c + 1) * 128]
        up = pltpu.roll(blk, 128 - half, axis=1)
        down = pltpu.roll(blk, half, axis=1)
        blocks.append(jnp.where(first, up, down))
    return blocks[0] if len(blocks) == 1 else jnp.concatenate(blocks, axis=1)


def _normed_rope(z, half, g, gsw, cos, sin, seg):
    sq = z * z
    hi = sq.astype(BF16)
    lo = (sq - hi.astype(F32)).astype(BF16)
    ssq = jnp.dot(hi, seg, preferred_element_type=F32) + jnp.dot(lo, seg, preferred_element_type=F32)
    r = lax.rsqrt(ssq * (0.5 / half) + EPS)
    return r * (z * (g * cos) + _swap_rotary_halves(z, half) * (gsw * sin))


def _proj_kernel(x_ref, mod_ref, gn_ref, w_ref, rope_ref, gain_ref, seg_ref, rep_ref,
                 za_ref, zb_ref, qc_ref, kc_ref, vc_ref, qdt_ref, kd_ref, vdt_ref, k8_ref):
    mod = mod_ref[...]
    h = _modulated_norm(x_ref[...], gn_ref[...], mod[0:1], mod[1:2])
    hb = h.astype(BF16)

    def mm(c0, c1):
        return jnp.dot(hb, w_ref[:, c0:c1], preferred_element_type=F32)

    z_cq, z_ck, z_dq, z_dk = mm(_CQ, _CK), mm(_CK, _CV), mm(_DQ, _DK), mm(_DK, _DV)
    za_ref[...] = mm(_A0, _B0).astype(BF16)
    zb_ref[...] = mm(_B0, _CQ).astype(BF16)
    vc_ref[...] = mm(_CV, _DQ).astype(BF16)

    vt = mm(_DV, D_IN).astype(BF16).astype(F32).T
    ones_blk = (lax.broadcasted_iota(jnp.int32, (VA_D - V_D, TK_D), 0) == 0).astype(F32)
    for kt in range(TM // TK_D):
        rows = []
        for hd in range(N_HEADS_D):
            rows += [vt[hd * V_D:(hd + 1) * V_D, kt * TK_D:(kt + 1) * TK_D], ones_blk]
        vdt_ref[kt] = jnp.concatenate(rows, axis=0).astype(BF16)

    rope = rope_ref[...]
    cos_c, sin_c, cos_d, sin_d = (rope[:, 0:128], rope[:, 128:256], rope[:, 256:384], rope[:, 384:512])
    two = lambda t: jnp.concatenate([t, t], axis=1)
    gains = gain_ref[...]
    seg64 = seg_ref[0]
    seg32 = seg_ref[1]
    qc_ref[...] = _normed_rope(z_cq, HEAD_C // 2, gains[0:1], gains[1:2], two(cos_c), two(sin_c), seg64).astype(BF16)
    kc_ref[...] = _normed_rope(z_ck, HEAD_C // 2, gains[2:3, 0:128], gains[3:4, 0:128],
                               cos_c, sin_c, seg64[0:128, 0:128]).astype(BF16)
    qd = _normed_rope(z_dq, HEAD_D // 2, gains[4:5], gains[5:6], two(cos_d), two(sin_d), seg32)
    qdt_ref[...] = qd.T.astype(BF16)
    kd = _normed_rope(z_dk, HEAD_D // 2, gains[6:7], gains[7:8], two(cos_d), two(sin_d), seg32).astype(BF16)
    kd_ref[...] = kd
    k4 = jnp.dot(kd, rep_ref[...], preferred_element_type=F32) * (1.0 / F8_Q_SCALE)
    hi = k4.astype(F8).astype(F32)
    lo = (k4 - hi).astype(F8).astype(F32)
    lane = lax.broadcasted_iota(jnp.int32, k4.shape, 1) % 128
    k8_ref[...] = jnp.where(lane < 2 * HEAD_D, hi, lo).astype(F8)


def _proj(x2, mod, gn, w_in, rope, gains, seg, rep, seq):
    t = x2.shape[0]
    tiles_per_seq = seq // TM
    row = lambda i: (i, 0)
    bf = lambda shape: jax.ShapeDtypeStruct(shape, BF16)
    widths = (768, 512, 256, 128, 128)
    return pl.pallas_call(
        _proj_kernel,
        grid=(t // TM,),
        in_specs=[
            pl.BlockSpec((TM, D_MODEL), row),
            pl.BlockSpec((None, 8, D_MODEL), lambda i: (i // tiles_per_seq, 0, 0)),
            _const_spec((1, D_MODEL)),
            _const_spec((D_MODEL, D_IN)),
            pl.BlockSpec((TM, 512), lambda i: (i % tiles_per_seq, 0)),
            _const_spec((8, 256)),
            _const_spec((2, 256, 256)),
            _const_spec((256, 1024)),
        ],
        out_specs=[pl.BlockSpec((TM, w), row) for w in widths] + [
            pl.BlockSpec((256, TM), lambda i: (0, i)),
            pl.BlockSpec((TM, 256), row),
            pl.BlockSpec((TM // TK_D, N_HEADS_D * VA_D, TK_D), lambda i: (i, 0, 0)),
            pl.BlockSpec((TM, 1024), row),
        ],
        out_shape=[bf((t, w)) for w in widths] + [
            bf((256, t)), bf((t, 256)), bf((t // TK_D, N_HEADS_D * VA_D, TK_D)), jax.ShapeDtypeStruct((t, 1024), F8)],
        compiler_params=_cparams(("parallel",)),
        name="proj",
    )(x2, mod, gn, w_in, rope, gains, seg, rep)


def _dwconv_rows(load, weights, offsets, base, n_out):
    nt = n_out // 8
    t_lo = (base + min(offsets)) // 8
    t_hi = (base + max(offsets) + n_out - 1) // 8 + 1
    x = load(t_lo * 8, t_hi * 8)
    c = x.shape[1]
    x = x.reshape(t_hi - t_lo, 8, c)
    sub = lax.broadcasted_iota(jnp.int32, (nt, 8, c), 1)
    out = None
    for b in range(8):
        taps = [k for k, off in enumerate(offsets) if (base + off) % 8 == b]
        if not taps:
            continue
        xr = x if b == 0 else pltpu.roll(x, 8 - b, axis=1)
        ext = nt + (1 if b else 0)
        acc = None
        for k in taps:
            a = (base + offsets[k]) // 8 - t_lo
            term = weights[k] * xr[a:a + ext]
            acc = term if acc is None else acc + term
        part = acc if b == 0 else jnp.where(sub < 8 - b, acc[:-1], acc[1:])
        out = part if out is None else out + part
    return out.reshape(n_out, c)


def _mix_ab_kernel(zap_ref, za_ref, zan_ref, zbp_ref, zb_ref, zbn_ref, ca_ref, cb_ref, bvec_ref,
                   ya_ref, yb_ref, abuf, bbuf, *, tiles_per_seq):
    i = pl.program_id(0)
    pos = i % tiles_per_seq
    keep_prev = (pos != 0).astype(F32)
    keep_next = (pos != tiles_per_seq - 1).astype(F32)

    def a_in(z):
        z = z.astype(F32)
        return z[:, 256:512] * z[:, 512:768]

    def b_in(z):
        z = z.astype(F32)
        return z[:, 0:256] * jax.nn.sigmoid(z[:, 256:512])

    abuf[0:HALO, :] = a_in(zap_ref[...]) * keep_prev
    abuf[HALO:HALO + TM, :] = a_in(za_ref[...])
    abuf[HALO + TM:, :] = a_in(zan_ref[...]) * keep_next
    bbuf[0:HALO, :] = b_in(zbp_ref[...]) * keep_prev
    bbuf[HALO:HALO + TM, :] = b_in(zb_ref[...])
    bbuf[HALO + TM:, :] = b_in(zbn_ref[...]) * keep_next

    ca = ca_ref[...]
    conv_a = _dwconv_rows(lambda r0, r1: abuf[r0:r1, :], [ca[k:k + 1] for k in range(3)], (-1, 0, 1), HALO, TM)
    ya_ref[...] = (za_ref[:, 0:256].astype(F32) * conv_a).astype(BF16)

    cb = cb_ref[...]
    pad = (CONF_K - 1) // 2
    t = _dwconv_rows(lambda r0, r1: bbuf[r0:r1, :], [cb[k:k + 1] for k in range(CONF_K)],
                     tuple(range(-pad, pad + 1)), HALO, TM)
    bvec = bvec_ref[...]
    t = t + bvec[0:1]
    mu = jnp.mean(t, axis=-1, keepdims=True)
    tc = t - mu
    var = jnp.mean(tc * tc, axis=-1, keepdims=True)
    y = tc * lax.rsqrt(var + EPS) * bvec[1:2] + bvec[2:3]
    yb_ref[...] = (y * jax.nn.sigmoid(y)).astype(BF16)


def _mix_ab(za, zb, conv_a, conv_b, bvec, seq):
    t = za.shape[0]
    tiles_per_seq = seq // TM
    r = TM // HALO
    nh = t // HALO
    row = lambda i: (i, 0)
    prev = lambda i: (jnp.maximum(i * r - 1, 0), 0)
    nxt = lambda i: (jnp.minimum((i + 1) * r, nh - 1), 0)
    return pl.pallas_call(
        functools.partial(_mix_ab_kernel, tiles_per_seq=tiles_per_seq),
        grid=(t // TM,),
        in_specs=[
            pl.BlockSpec((HALO, 768), prev), pl.BlockSpec((TM, 768), row), pl.BlockSpec((HALO, 768), nxt),
            pl.BlockSpec((HALO, 512), prev), pl.BlockSpec((TM, 512), row), pl.BlockSpec((HALO, 512), nxt),
            _const_spec((8, 256)), _const_spec((32, 256)), _const_spec((8, 256)),
        ],
        out_specs=[pl.BlockSpec((TM, BRANCH), row), pl.BlockSpec((TM, BRANCH), row)],
        out_shape=[jax.ShapeDtypeStruct((t, BRANCH), BF16)] * 2,
        scratch_shapes=[pltpu.VMEM((TM + 2 * HALO, BRANCH), F32), pltpu.VMEM((TM + 2 * HALO, BRANCH), F32)],
        compiler_params=_cparams(("parallel",)),
        name="mix_ab",
    )(za, za, za, zb, zb, zb, conv_a, conv_b, bvec)


def _attn_c_kernel(q_ref, kp_ref, k_ref, kn_ref, vp_ref, v_ref, vn_ref, sink_ref, o_ref, s_ref, p_ref, ot_ref,
                   *, tiles_per_seq):
    i = pl.program_id(0)
    pos = i % tiles_per_seq
    first = pos == 0
    last = pos == tiles_per_seq - 1
    nsub = TQ_C // WINDOW

    kk = jnp.concatenate([kp_ref[...], k_ref[...], kn_ref[...]], axis=0)
    vt = jnp.concatenate([vp_ref[...], v_ref[...], vn_ref[...]], axis=0).astype(F32).T.astype(BF16)
    qt = q_ref[...].astype(F32).T
    zeros = jnp.zeros((HEAD_C, TQ_C), F32)
    sink = sink_ref[...]
    tt = lax.broadcasted_iota(jnp.int32, (3 * WINDOW, WINDOW), 0)
    aa = lax.broadcasted_iota(jnp.int32, (3 * WINDOW, WINDOW), 1)
    rel_ok = jnp.abs(tt - WINDOW - aa) <= WINDOW

    problems = [(j, h) for j in range(nsub) for h in range(4)]
    qz = []
    for h in range(4):
        qh = qt[h * HEAD_C:(h + 1) * HEAD_C, :]
        qz.append(jnp.concatenate([qh, zeros] if h // 2 == 0 else [zeros, qh], axis=0).astype(BF16))
    for n, (j, h) in enumerate(problems):
        r0 = j * WINDOW
        s_ref[n] = jnp.dot(kk[r0:r0 + 3 * WINDOW, :], qz[h][:, r0:r0 + WINDOW], preferred_element_type=F32)
    for n, (j, h) in enumerate(problems):
        ok = rel_ok
        if j == 0:
            ok = ok & (tt >= jnp.where(first, WINDOW, 0))
        if j == nsub - 1:
            ok = ok & (tt < jnp.where(last, 2 * WINDOW, 3 * WINDOW))
        s = jnp.where(ok, s_ref[n], NEG)
        sk = sink[h:h + 1, :]
        m = jnp.maximum(jnp.max(s, axis=0, keepdims=True), sk)
        p = jnp.exp2(s - m)
        den = jnp.sum(p, axis=0, keepdims=True) + jnp.exp2(sk - m)
        p_ref[n] = (p / den).astype(BF16)
    for n, (j, h) in enumerate(problems):
        r0 = j * WINDOW
        vh = vt[(h // 2) * HEAD_C:(h // 2 + 1) * HEAD_C, r0:r0 + 3 * WINDOW]
        ot_ref[h * HEAD_C:(h + 1) * HEAD_C, r0:r0 + WINDOW] = jnp.dot(vh, p_ref[n], preferred_element_type=F32)
    o_ref[...] = ot_ref[...].T.astype(BF16)


def _attn_c(qc, kc, vc, sink, seq):
    t = qc.shape[0]
    tiles_per_seq = seq // TQ_C
    r = TQ_C // WINDOW
    nh = t // WINDOW
    row = lambda i: (i, 0)
    prev = lambda i: (jnp.maximum(i * r - 1, 0), 0)
    nxt = lambda i: (jnp.minimum((i + 1) * r, nh - 1), 0)
    kv_specs = [pl.BlockSpec((WINDOW, 128), prev), pl.BlockSpec((TQ_C, 128), row), pl.BlockSpec((WINDOW, 128), nxt)]
    return pl.pallas_call(
        functools.partial(_attn_c_kernel, tiles_per_seq=tiles_per_seq),
        grid=(t // TQ_C,),
        in_specs=[pl.BlockSpec((TQ_C, BRANCH), row)] + kv_specs + kv_specs + [_const_spec((8, 128))],
        out_specs=pl.BlockSpec((TQ_C, BRANCH), row),
        out_shape=jax.ShapeDtypeStruct((t, BRANCH), BF16),
        scratch_shapes=[pltpu.VMEM((4 * r, 3 * WINDOW, WINDOW), F32), pltpu.VMEM((4 * r, 3 * WINDOW, WINDOW), BF16),
                        pltpu.VMEM((BRANCH, TQ_C), F32)],
        compiler_params=_cparams(("parallel",)),
        name="attn_c",
    )(qc, kc, kc, kc, vc, vc, vc, sink)


def _attn_d_kernel(f8ok_ref, lam_ref, gs_ref, qt_ref, k_ref, k8_ref, vt_ref, o_ref, s_ref, mx_ref, m_ref, acc_ref,
                   ot_ref, *, nkt, lam_init):
    qt = qt_ref[...].astype(F32)
    m_ref[...] = jnp.full(m_ref.shape, NEG, F32)
    acc_ref[...] = jnp.zeros(acc_ref.shape, F32)

    def consume(vt, slot, g):
        h = g // 2
        m_old = m_ref[g]
        m_new = jnp.maximum(m_old, jnp.max(mx_ref[slot, g], axis=0, keepdims=True))
        alpha = jnp.exp2(m_old - m_new)
        p = jnp.exp2(s_ref[slot, g] - m_new[0:1])
        pv = jnp.dot(vt[h * VA_D:(h + 1) * VA_D, :], p.astype(BF16), preferred_element_type=F32)
        acc_ref[g] = acc_ref[g] * alpha[0:1] + pv
        m_ref[g] = m_new

    def run(keys_ref, score):
        def scores(kk, slot, g):
            s = score(kk, g)
            s_ref[slot, g] = s
            mx_ref[slot, g] = jnp.max(s.reshape(TK_D // 8, 8, TQ_D), axis=0)

        kk0 = keys_ref[0]
        for g in range(8):
            scores(kk0, 0, g)

        def body(u, carry):
            for j in range(UNROLL_D):
                kt = UNROLL_D * u + j
                kk = keys_ref[jnp.minimum(kt + 1, nkt - 1)]
                vt = vt_ref[kt]
                for g in range(8):
                    scores(kk, (j + 1) % 2, g)
                    consume(vt, j % 2, g)
            return carry

        lax.fori_loop(0, nkt // UNROLL_D, body, 0)

    @pl.when(f8ok_ref[0] != 0)
    def _():
        q8 = []
        for g in range(8):
            qg = qt[g * HEAD_D:(g + 1) * HEAD_D, :] * F8_Q_SCALE
            hi = qg.astype(F8).astype(F32)
            lo = (qg - hi).astype(F8).astype(F32)
            q8.append(jnp.concatenate([hi, lo, hi, lo], axis=0).astype(F8))
        run(k8_ref, lambda kk, g: jnp.dot(kk[:, g * 128:(g + 1) * 128], q8[g], preferred_element_type=F32))

    @pl.when(f8ok_ref[0] == 0)
    def _():
        grp = lax.broadcasted_iota(jnp.int32, (128, TQ_D), 0) // HEAD_D
        qz = []
        for g in range(8):
            half = qt[(g // 4) * 128:(g // 4 + 1) * 128, :]
            qz.append(jnp.where(grp == (g % 4), half, 0.0).astype(BF16))
        run(k_ref, lambda kk, g: jnp.dot(kk[:, (g // 4) * 128:(g // 4 + 1) * 128], qz[g],
                                         preferred_element_type=F32))

    lf = lam_ref[...]
    lam = (jnp.exp(jnp.sum(lf[0:1] * lf[1:2], axis=-1, keepdims=True))
           - jnp.exp(jnp.sum(lf[2:3] * lf[3:4], axis=-1, keepdims=True)) + lam_init)
    gs = gs_ref[...]
    for h in range(N_HEADS_D):
        a1 = acc_ref[2 * h]
        a2 = acc_ref[2 * h + 1]
        o = a1[0:V_D] / a1[V_D:V_D + 1] - lam * (a2[0:V_D] / a2[V_D:V_D + 1])
        ms = jnp.mean(o * o, axis=0, keepdims=True)
        ot_ref[h * V_D:(h + 1) * V_D, :] = o * lax.rsqrt(ms + EPS) * gs * (1.0 - lam_init)
    o_ref[...] = ot_ref[...].T.astype(BF16)


def _attn_d(f8ok, qdt, kd, k8, vdt, lam_d, gs, lam_init, b, s):
    nkt = s // TK_D
    nqt = s // TQ_D
    assert nkt % UNROLL_D == 0 and UNROLL_D % 2 == 0
    resident = lambda shape: pl.BlockSpec((None,) + shape, lambda bi, qi: (bi, 0, 0, 0), pipeline_mode=pl.Buffered(1))
    return pl.pallas_call(
        functools.partial(_attn_d_kernel, nkt=nkt, lam_init=lam_init),
        grid=(b, nqt),
        in_specs=[
            pl.BlockSpec(memory_space=pltpu.SMEM),
            _const_spec((4, HEAD_D)),
            _const_spec((V_D, TQ_D)),
            pl.BlockSpec((256, TQ_D), lambda bi, qi: (0, bi * nqt + qi)),
            resident((nkt, TK_D, 256)),
            resident((nkt, TK_D, 1024)),
            resident((nkt, N_HEADS_D * VA_D, TK_D)),
        ],
        out_specs=pl.BlockSpec((TQ_D, 256), lambda bi, qi: (bi * nqt + qi, 0)),
        out_shape=jax.ShapeDtypeStruct((b * s, 256), BF16),
        scratch_shapes=[pltpu.VMEM((2, 8, TK_D, TQ_D), F32), pltpu.VMEM((2, 8, 8, TQ_D), F32),
                        pltpu.VMEM((8, 8, TQ_D), F32), pltpu.VMEM((8, VA_D, TQ_D), F32),
                        pltpu.VMEM((N_HEADS_D * V_D, TQ_D), F32)],
        compiler_params=_cparams(("parallel", "arbitrary")),
        name="attn_d",
    )(f8ok, lam_d, gs, qdt, kd.reshape(b, nkt, TK_D, 256), k8.reshape(b, nkt, TK_D, 1024),
      vdt.reshape(b, nkt, N_HEADS_D * VA_D, TK_D))


def _merge_kernel(x_ref, mod_ref, gn_ref, ya_ref, yb_ref, yc_ref, yd_ref, wm_ref, wb_ref, wo_ref, o_ref):
    mod = mod_ref[...]
    gn = gn_ref[...]
    half = TM // 2

    def gated(r0):
        hb = _modulated_norm(x_ref[r0:r0 + half, :], gn, mod[0:1], mod[1:2]).astype(BF16)
        merged = None
        for i, y_ref in enumerate((ya_ref, yb_ref, yc_ref, yd_ref)):
            gate = jax.nn.sigmoid(jnp.dot(hb, wm_ref[i], preferred_element_type=F32))
            term = gate * jnp.dot(y_ref[r0:r0 + half, :], wb_ref[i], preferred_element_type=F32)
            merged = term if merged is None else merged + term
        return merged.astype(BF16)

    halves = [gated(0), gated(half)]
    for n, r0 in enumerate((0, half)):
        out = jnp.dot(halves[n], wo_ref[...], preferred_element_type=F32)
        o_ref[r0:r0 + half, :] = x_ref[r0:r0 + half, :] + mod[2:3] * out


def _merge(x2, mod, gn, ya, yb, yc, yd, wm, wb, wo, seq):
    t = x2.shape[0]
    tiles_per_seq = seq // TM
    row = lambda i: (i, 0)
    return pl.pallas_call(
        _merge_kernel,
        grid=(t // TM,),
        in_specs=[
            pl.BlockSpec((TM, D_MODEL), row),
            pl.BlockSpec((None, 8, D_MODEL), lambda i: (i // tiles_per_seq, 0, 0)),
            _const_spec((1, D_MODEL)),
        ] + [pl.BlockSpec((TM, BRANCH), row)] * 4 + [
            _const_spec((4, D_MODEL, D_MODEL)),
            _const_spec((4, BRANCH, D_MODEL)),
            _const_spec((D_MODEL, D_MODEL)),
        ],
        out_specs=pl.BlockSpec((TM, D_MODEL), row),
        out_shape=jax.ShapeDtypeStruct((t, D_MODEL), F32),
        compiler_params=_cparams(("parallel",)),
        name="merge",
    )(x2, mod, gn, ya, yb, yc, yd, wm, wb, wo)


def _ffn_kernel(xp_ref, x_ref, xn_ref, mod_ref, gn_ref, wup_ref, cw_ref, wdn_ref, o_ref, hbuf, ug, uv, act, acc,
                *, tiles_per_seq):
    i = pl.program_id(0)
    pos = i % tiles_per_seq
    keep_prev = (pos != 0).astype(F32)
    keep_next = (pos != tiles_per_seq - 1).astype(F32)
    mod = mod_ref[...]
    gn = gn_ref[...]
    x = x_ref[...]
    norm = lambda v: _modulated_norm(v, gn, mod[3:4], mod[4:5])
    hbuf[0:HALO, :] = (norm(xp_ref[...]) * keep_prev).astype(BF16)
    hbuf[HALO:HALO + TM, :] = norm(x).astype(BF16)
    hbuf[HALO + TM:, :] = (norm(xn_ref[...]) * keep_next).astype(BF16)
    hb = hbuf[...]

    def conv(buf, slot, r0, width, col):
        w = cw_ref[:, col:col + width]
        return _dwconv_rows(lambda a, b: buf[slot, a:b, 0:width], [w[k:k + 1] for k in range(3)], (-1, 0, 1),
                            HALO + r0, FF_ROWS) + w[3:4]

    starts = [sum(FF_CHUNKS[:c]) for c in range(len(FF_CHUNKS))]

    def up(c):
        c0, width, slot = starts[c], FF_CHUNKS[c], c % 2
        ug[slot, :, 0:width] = jnp.dot(hb, wup_ref[:, c0:c0 + width], preferred_element_type=F32)
        uv[slot, :, 0:width] = jnp.dot(hb, wup_ref[:, D_FF + c0:D_FF + c0 + width], preferred_element_type=F32)

    up(0)
    for c, (c0, width) in enumerate(zip(starts, FF_CHUNKS)):
        if c + 1 < len(FF_CHUNKS):
            up(c + 1)
        for r0 in range(0, TM, FF_ROWS):
            g = conv(ug, c % 2, r0, width, c0)
            v = conv(uv, c % 2, r0, width, D_FF + c0)
            act[c % 2, r0:r0 + FF_ROWS, 0:width] = (g * jax.nn.sigmoid(g) * v).astype(BF16)
        part = jnp.dot(act[c % 2, :, 0:width], wdn_ref[c0:c0 + width, :], preferred_element_type=F32)
        if c == 0:
            acc[...] = part
        else:
            acc[...] += part
    o_ref[...] = x + mod[5:6] * acc[...]


def _ffn(x2, mod, gn, wup, cw, wdn, seq):
    t = x2.shape[0]
    tiles_per_seq = seq // TM
    r = TM // HALO
    nh = t // HALO
    row = lambda i: (i, 0)
    prev = lambda i: (jnp.maximum(i * r - 1, 0), 0)
    nxt = lambda i: (jnp.minimum((i + 1) * r, nh - 1), 0)
    chunk = max(FF_CHUNKS)
    return pl.pallas_call(
        functools.partial(_ffn_kernel, tiles_per_seq=tiles_per_seq),
        grid=(t // TM,),
        in_specs=[
            pl.BlockSpec((HALO, D_MODEL), prev), pl.BlockSpec((TM, D_MODEL), row), pl.BlockSpec((HALO, D_MODEL), nxt),
            pl.BlockSpec((None, 8, D_MODEL), lambda i: (i // tiles_per_seq, 0, 0)),
            _const_spec((1, D_MODEL)),
            _const_spec((D_MODEL, 2 * D_FF)),
            _const_spec((8, 2 * D_FF)),
            _const_spec((D_FF, D_MODEL)),
        ],
        out_specs=pl.BlockSpec((TM, D_MODEL), row),
        out_shape=jax.ShapeDtypeStruct((t, D_MODEL), F32),
        scratch_shapes=[pltpu.VMEM((TM + 2 * HALO, D_MODEL), BF16),
                        pltpu.VMEM((2, TM + 2 * HALO, chunk), F32), pltpu.VMEM((2, TM + 2 * HALO, chunk), F32),
                        pltpu.VMEM((2, TM, chunk), BF16), pltpu.VMEM((TM, D_MODEL), F32)],
        compiler_params=_cparams(("parallel",)),
        name="ffn",
    )(x2, x2, x2, mod, gn, wup, cw, wdn)


def _swap_halves(a, head_dim):
    shp = a.shape
    a = a.reshape(shp[:-1] + (shp[-1] // head_dim, 2, head_dim // 2))
    return a[..., ::-1, :].reshape(shp)


def _rope_table(seq):
    pos = jnp.arange(seq, dtype=F32)[:, None]

    def one(dim):
        inv = 1.0 / (ROPE_THETA ** (jnp.arange(0, dim, 2, dtype=F32) / dim))
        ang = pos * inv[None, :]
        c, s = jnp.cos(ang), jnp.sin(ang)
        rep = 128 // dim
        return jnp.tile(jnp.concatenate([c, c], 1), (1, rep)), jnp.tile(jnp.concatenate([-s, s], 1), (1, rep))

    cc, sc = one(HEAD_C)
    cd, sd = one(HEAD_D)
    return jnp.concatenate([cc, sc, cd, sd], axis=1)


def _layer_params(l, w_in, conv_a, conv_b, conv_b_bias, ln_b_gain, ln_b_bias, g_q_c, g_k_c, sink_c, g_q_d, g_k_d,
                  g_subln_d, w_merge, w_branch, w_out, w_up, conv_ffn, conv_ffn_bias, w_down):
    def tiled(g, scale):
        g = g.astype(F32) * scale
        rep = 256 // g.shape[0]
        return jnp.tile(g, rep), jnp.tile(_swap_halves(g, g.shape[0]), rep)

    rows = (tiled(g_q_c[l], HEAD_C ** -0.5 * LOG2E) + tiled(g_k_c[l], 1.0)
            + tiled(g_q_d[l], HEAD_D ** -0.5 * LOG2E) + tiled(g_k_d[l], 1.0))
    gains = jnp.stack(rows, axis=0)
    pad_rows = lambda a, n: jnp.pad(a.astype(F32), ((0, n - a.shape[0]), (0, 0)))
    return dict(
        w_in=w_in[l].astype(BF16), gains=gains,
        f8ok=((jnp.max(jnp.abs(g_q_d[l])) <= F8_MAX_GAIN) & (jnp.max(jnp.abs(g_k_d[l])) <= F8_MAX_GAIN)
              ).astype(jnp.int32).reshape(1),
        conv_a=pad_rows(conv_a[l], 8), conv_b=pad_rows(conv_b[l], 32),
        bvec=pad_rows(jnp.stack([conv_b_bias[l], ln_b_gain[l], ln_b_bias[l]], 0), 8),
        sink=jnp.broadcast_to(pad_rows(sink_c[l][:, None] * LOG2E, 8), (8, 128)),
        gs=jnp.broadcast_to(g_subln_d[l].astype(F32)[:, None], (V_D, TQ_D)),
        wm=w_merge[l].astype(BF16), wb=w_branch[l].astype(BF16), wo=w_out[l].astype(BF16),
        wup=w_up[l].astype(BF16), wdn=w_down[l].astype(BF16),
        cw=pad_rows(jnp.concatenate([conv_ffn[l], conv_ffn_bias[l][None, :]], 0), 8),
    )


def _segment_ones():
    idx = jnp.arange(256)
    seg = lambda n: (idx[:, None] // n == idx[None, :] // n).astype(BF16)
    return jnp.stack([seg(HEAD_C), seg(HEAD_D)], 0)


def _replicate4():
    col = jnp.arange(1024)
    src = (col // 128) * HEAD_D + col % HEAD_D
    return (jnp.arange(256)[:, None] == src[None, :]).astype(BF16)


def _trunk(x, mods, layers, seg, rep, g_norm_mix, g_norm_ffn, lam_d):
    b, s, d = x.shape
    x2 = x.reshape(b * s, d)
    rope = _rope_table(s)
    for l in range(DEPTH):
        p = layers[l]
        gn1 = g_norm_mix[l].astype(F32)[None, :]
        gn2 = g_norm_ffn[l].astype(F32)[None, :]
        za, zb, qc, kc, vc, qdt, kd, vdt, k8 = _proj(x2, mods[l], gn1, p["w_in"], rope, p["gains"], seg, rep, s)
        ya, yb = _mix_ab(za, zb, p["conv_a"], p["conv_b"], p["bvec"], s)
        yc = _attn_c(qc, kc, vc, p["sink"], s)
        lam_init = 0.8 - 0.6 * math.exp(-0.3 * l)
        yd = _attn_d(p["f8ok"], qdt, kd, k8, vdt, lam_d[l].astype(F32), p["gs"], lam_init, b, s)
        x2 = _merge(x2, mods[l], gn1, ya, yb, yc, yd, p["wm"], p["wb"], p["wo"], s)
        x2 = _ffn(x2, mods[l], gn2, p["wup"], p["cw"], p["wdn"], s)
    return x2.reshape(b, s, d)


def kernel(x_prompt, x_sample, c_prompt, c_sample, w_ada, b_ada, g_norm_mix, w_in, conv_a, conv_b, conv_b_bias,
           ln_b_gain, ln_b_bias, g_q_c, g_k_c, sink_c, g_q_d, g_k_d, lam_d, g_subln_d, w_merge, w_branch, w_out,
           g_norm_ffn, w_up, conv_ffn, conv_ffn_bias, w_down):
    bp, bs = x_prompt.shape[0], x_sample.shape[0]
    assert bp + bs <= 8
    c8 = jnp.concatenate([c_prompt, c_sample, jnp.zeros((8 - bp - bs, D_MODEL), F32)], axis=0)
    mod = _ada(c8, w_ada, b_ada)
    mod = mod.reshape(DEPTH, 8, N_ADA, D_MODEL)
    mod = jnp.pad(mod, ((0, 0), (0, 0), (0, 8 - N_ADA), (0, 0)))
    layers = [_layer_params(l, w_in, conv_a, conv_b, conv_b_bias, ln_b_gain, ln_b_bias, g_q_c, g_k_c, sink_c, g_q_d,
                            g_k_d, g_subln_d, w_merge, w_branch, w_out, w_up, conv_ffn, conv_ffn_bias, w_down)
              for l in range(DEPTH)]
    seg = _segment_ones()
    rep = _replicate4()
    y_prompt = _trunk(x_prompt, [mod[l, :bp] for l in range(DEPTH)], layers, seg, rep, g_norm_mix, g_norm_ffn, lam_d)
    y_sample = _trunk(x_sample, [mod[l, bp:bp + bs] for l in range(DEPTH)], layers, seg, rep, g_norm_mix,
                      g_norm_ffn, lam_d)
    return (y_prompt, y_sample)
```

```python
import functools
import math

import jax
import jax.numpy as jnp
from jax import lax
from jax.experimental import pallas as pl
from jax.experimental.pallas import tpu as pltpu

F32 = jnp.float32
BF16 = jnp.bfloat16
F8 = jnp.float8_e4m3fn

D_MODEL = 1024
DEPTH = 2
BRANCH = 256
HEAD_C = 64
WINDOW = 128
HEAD_D = 32
V_D = 64
VA_D = 80
N_HEADS_D = 4
F8_Q_SCALE = 2.0
F8_MAX_GAIN = 100.0
CONF_K = 31
D_FF = 2816
D_IN = 2560
N_ADA = 6
ROPE_THETA = 10000.0
EPS = 1e-6
NEG = -1e30
LOG2E = 1.4426950408889634

VMEM_LIMIT = 56 * 1024 * 1024

TM = 512
HALO = 16
TQ_C = 512
TQ_D = 256
TK_D = 256
UNROLL_D = 16
FF_CHUNKS = (512, 512, 512, 512, 512, 256)
FF_ROWS = 128


def _cparams(sem):
    return pltpu.CompilerParams(dimension_semantics=sem, vmem_limit_bytes=VMEM_LIMIT)


def _const_spec(shape):
    nd = len(shape)
    return pl.BlockSpec(shape, lambda *_: (0,) * nd, pipeline_mode=pl.Buffered(1))


def _modulated_norm(x, gain, shift, scale):
    ms = jnp.mean(x * x, axis=-1, keepdims=True)
    return x * lax.rsqrt(ms + EPS) * gain * (1.0 + scale) + shift


def _ada_kernel(c_ref, w_ref, b_ref, o_ref):
    c = c_ref[...]
    a = c * jax.nn.sigmoid(c)
    a_hi = a.astype(BF16)
    a_lo = (a - a_hi.astype(F32)).astype(BF16)
    w = w_ref[...]
    w_hi = w.astype(BF16)
    w_lo = (w - w_hi.astype(F32)).astype(BF16)
    acc = jnp.dot(a_hi, w_hi, preferred_element_type=F32)
    acc += jnp.dot(a_lo, w_hi, preferred_element_type=F32)
    acc += jnp.dot(a_hi, w_lo, preferred_element_type=F32)
    o_ref[...] = acc + b_ref[...]


def _ada(c8, w_ada, b_ada):
    tn = 1536
    n = N_ADA * D_MODEL
    return pl.pallas_call(
        _ada_kernel,
        grid=(DEPTH, n // tn),
        in_specs=[
            pl.BlockSpec((8, D_MODEL), lambda l, j: (0, 0)),
            pl.BlockSpec((None, D_MODEL, tn), lambda l, j: (l, 0, j)),
            pl.BlockSpec((None, 1, tn), lambda l, j: (l, 0, j)),
        ],
        out_specs=pl.BlockSpec((None, 8, tn), lambda l, j: (l, 0, j)),
        out_shape=jax.ShapeDtypeStruct((DEPTH, 8, n), F32),
        compiler_params=_cparams(("arbitrary", "arbitrary")),
        name="ada",
    )(c8, w_ada, b_ada.reshape(DEPTH, 1, n))


_A0, _B0, _CQ, _CK, _CV, _DQ, _DK, _DV = 0, 768, 1280, 1536, 1664, 1792, 2048, 2304


def _swap_rotary_halves(z, half):
    lane = lax.broadcasted_iota(jnp.int32, (z.shape[0], 128), 1)
    first = (lane % (2 * half)) < half
    blocks = []
    for c in range(z.shape[1] // 128):
        blk = z[:, c * 128:(c + 1) * 128]
        up = pltpu.roll(blk, 128 - half, axis=1)
        down = pltpu.roll(blk, half, axis=1)
        blocks.append(jnp.where(first, up, down))
    return blocks[0] if len(blocks) == 1 else jnp.concatenate(blocks, axis=1)


def _normed_rope(z, half, g, gsw, cos, sin, seg):
    sq = z * z
    hi = sq.astype(BF16)
    lo = (sq - hi.astype(F32)).astype(BF16)
    ssq = jnp.dot(hi, seg, preferred_element_type=F32) + jnp.dot(lo, seg, preferred_element_type=F32)
    r = lax.rsqrt(ssq * (0.5 / half) + EPS)
    return r * (z * (g * cos) + _swap_rotary_halves(z, half) * (gsw * sin))


def _proj_kernel(x_ref, mod_ref, gn_ref, w_ref, rope_ref, gain_ref, seg_ref, rep_ref,
                 za_ref, zb_ref, qc_ref, kc_ref, vc_ref, qdt_ref, kd_ref, vdt_ref, k8_ref):
    mod = mod_ref[...]
    h = _modulated_norm(x_ref[...], gn_ref[...], mod[0:1], mod[1:2])
    hb = h.astype(BF16)

    def mm(c0, c1):
        return jnp.dot(hb, w_ref[:, c0:c1], preferred_element_type=F32)

    z_cq, z_ck, z_dq, z_dk = mm(_CQ, _CK), mm(_CK, _CV), mm(_DQ, _DK), mm(_DK, _DV)
    za_ref[...] = mm(_A0, _B0).astype(BF16)
    zb_ref[...] = mm(_B0, _CQ).astype(BF16)
    vc_ref[...] = mm(_CV, _DQ).astype(BF16)

    vt = mm(_DV, D_IN).astype(BF16).astype(F32).T
    ones_blk = (lax.broadcasted_iota(jnp.int32, (VA_D - V_D, TK_D), 0) == 0).astype(F32)
    for kt in range(TM // TK_D):
        rows = []
        for hd in range(N_HEADS_D):
            rows += [vt[hd * V_D:(hd + 1) * V_D, kt * TK_D:(kt + 1) * TK_D], ones_blk]
        vdt_ref[kt] = jnp.concatenate(rows, axis=0).astype(BF16)

    rope = rope_ref[...]
    cos_c, sin_c, cos_d, sin_d = (rope[:, 0:128], rope[:, 128:256], rope[:, 256:384], rope[:, 384:512])
    two = lambda t: jnp.concatenate([t, t], axis=1)
    gains = gain_ref[...]
    seg64 = seg_ref[0]
    seg32 = seg_ref[1]
    qc_ref[...] = _normed_rope(z_cq, HEAD_C // 2, gains[0:1], gains[1:2], two(cos_c), two(sin_c), seg64).astype(BF16)
    kc_ref[...] = _normed_rope(z_ck, HEAD_C // 2, gains[2:3, 0:128], gains[3:4, 0:128],
                               cos_c, sin_c, seg64[0:128, 0:128]).astype(BF16)
    qd = _normed_rope(z_dq, HEAD_D // 2, gains[4:5], gains[5:6], two(cos_d), two(sin_d), seg32)
    qdt_ref[...] = qd.T.astype(BF16)
    kd = _normed_rope(z_dk, HEAD_D // 2, gains[6:7], gains[7:8], two(cos_d), two(sin_d), seg32).astype(BF16)
    kd_ref[...] = kd
    k4 = jnp.dot(kd, rep_ref[...], preferred_element_type=F32) * (1.0 / F8_Q_SCALE)
    hi = k4.astype(F8).astype(F32)
    lo = (k4 - hi).astype(F8).astype(F32)
    lane = lax.broadcasted_iota(jnp.int32, k4.shape, 1) % 128
    k8_ref[...] = jnp.where(lane < 2 * HEAD_D, hi, lo).astype(F8)


def _proj(x2, mod, gn, w_in, rope, gains, seg, rep, seq):
    t = x2.shape[0]
    tiles_per_seq = seq // TM
    row = lambda i: (i, 0)
    bf = lambda shape: jax.ShapeDtypeStruct(shape, BF16)
    widths = (768, 512, 256, 128, 128)
    return pl.pallas_call(
        _proj_kernel,
        grid=(t // TM,),
        in_specs=[
            pl.BlockSpec((TM, D_MODEL), row),
            pl.BlockSpec((None, 8, D_MODEL), lambda i: (i // tiles_per_seq, 0, 0)),
            _const_spec((1, D_MODEL)),
            _const_spec((D_MODEL, D_IN)),
            pl.BlockSpec((TM, 512), lambda i: (i % tiles_per_seq, 0)),
            _const_spec((8, 256)),
            _const_spec((2, 256, 256)),
            _const_spec((256, 1024)),
        ],
        out_specs=[pl.BlockSpec((TM, w), row) for w in widths] + [
            pl.BlockSpec((256, TM), lambda i: (0, i)),
            pl.BlockSpec((TM, 256), row),
            pl.BlockSpec((TM // TK_D, N_HEADS_D * VA_D, TK_D), lambda i: (i, 0, 0)),
            pl.BlockSpec((TM, 1024), row),
        ],
        out_shape=[bf((t, w)) for w in widths] + [
            bf((256, t)), bf((t, 256)), bf((t // TK_D, N_HEADS_D * VA_D, TK_D)), jax.ShapeDtypeStruct((t, 1024), F8)],
        compiler_params=_cparams(("parallel",)),
        name="proj",
    )(x2, mod, gn, w_in, rope, gains, seg, rep)


def _dwconv_rows(load, weights, offsets, base, n_out):
    nt = n_out // 8
    t_lo = (base + min(offsets)) // 8
    t_hi = (base + max(offsets) + n_out - 1) // 8 + 1
    x = load(t_lo * 8, t_hi * 8)
    c = x.shape[1]
    x = x.reshape(t_hi - t_lo, 8, c)
    sub = lax.broadcasted_iota(jnp.int32, (nt, 8, c), 1)
    out = None
    for b in range(8):
        taps = [k for k, off in enumerate(offsets) if (base + off) % 8 == b]
        if not taps:
            continue
        xr = x if b == 0 else pltpu.roll(x, 8 - b, axis=1)
        ext = nt + (1 if b else 0)
        acc = None
        for k in taps:
            a = (base + offsets[k]) // 8 - t_lo
            term = weights[k] * xr[a:a + ext]
            acc = term if acc is None else acc + term
        part = acc if b == 0 else jnp.where(sub < 8 - b, acc[:-1], acc[1:])
        out = part if out is None else out + part
    return out.reshape(n_out, c)


def _mix_ab_kernel(zap_ref, za_ref, zan_ref, zbp_ref, zb_ref, zbn_ref, ca_ref, cb_ref, bvec_ref,
                   ya_ref, yb_ref, abuf, bbuf, *, tiles_per_seq):
    i = pl.program_id(0)
    pos = i % tiles_per_seq
    keep_prev = (pos != 0).astype(F32)
    keep_next = (pos != tiles_per_seq - 1).astype(F32)

    def a_in(z):
        z = z.astype(F32)
        return z[:, 256:512] * z[:, 512:768]

    def b_in(z):
        z = z.astype(F32)
        return z[:, 0:256] * jax.nn.sigmoid(z[:, 256:512])

    abuf[0:HALO, :] = a_in(zap_ref[...]) * keep_prev
    abuf[HALO:HALO + TM, :] = a_in(za_ref[...])
    abuf[HALO + TM:, :] = a_in(zan_ref[...]) * keep_next
    bbuf[0:HALO, :] = b_in(zbp_ref[...]) * keep_prev
    bbuf[HALO:HALO + TM, :] = b_in(zb_ref[...])
    bbuf[HALO + TM:, :] = b_in(zbn_ref[...]) * keep_next

    ca = ca_ref[...]
    conv_a = _dwconv_rows(lambda r0, r1: abuf[r0:r1, :], [ca[k:k + 1] for k in range(3)], (-1, 0, 1), HALO, TM)
    ya_ref[...] = (za_ref[:, 0:256].astype(F32) * conv_a).astype(BF16)

    cb = cb_ref[...]
    pad = (CONF_K - 1) // 2
    t = _dwconv_rows(lambda r0, r1: bbuf[r0:r1, :], [cb[k:k + 1] for k in range(CONF_K)],
                     tuple(range(-pad, pad + 1)), HALO, TM)
    bvec = bvec_ref[...]
    t = t + bvec[0:1]
    mu = jnp.mean(t, axis=-1, keepdims=True)
    tc = t - mu
    var = jnp.mean(tc * tc, axis=-1, keepdims=True)
    y = tc * lax.rsqrt(var + EPS) * bvec[1:2] + bvec[2:3]
    yb_ref[...] = (y * jax.nn.sigmoid(y)).astype(BF16)


def _mix_ab(za, zb, conv_a, conv_b, bvec, seq):
    t = za.shape[0]
    tiles_per_seq = seq // TM
    r = TM // HALO
    nh = t // HALO
    row = lambda i: (i, 0)
    prev = lambda i: (jnp.maximum(i * r - 1, 0), 0)
    nxt = lambda i: (jnp.minimum((i + 1) * r, nh - 1), 0)
    return pl.pallas_call(
        functools.partial(_mix_ab_kernel, tiles_per_seq=tiles_per_seq),
        grid=(t // TM,),
        in_specs=[
            pl.BlockSpec((HALO, 768), prev), pl.BlockSpec((TM, 768), row), pl.BlockSpec((HALO, 768), nxt),
            pl.BlockSpec((HALO, 512), prev), pl.BlockSpec((TM, 512), row), pl.BlockSpec((HALO, 512), nxt),
            _const_spec((8, 256)), _const_spec((32, 256)), _const_spec((8, 256)),
        ],
        out_specs=[pl.BlockSpec((TM, BRANCH), row), pl.BlockSpec((TM, BRANCH), row)],
        out_shape=[jax.ShapeDtypeStruct((t, BRANCH), BF16)] * 2,
        scratch_shapes=[pltpu.VMEM((TM + 2 * HALO, BRANCH), F32), pltpu.VMEM((TM + 2 * HALO, BRANCH), F32)],
        compiler_params=_cparams(("parallel",)),
        name="mix_ab",
    )(za, za, za, zb, zb, zb, conv_a, conv_b, bvec)


def _attn_c_kernel(q_ref, kp_ref, k_ref, kn_ref, vp_ref, v_ref, vn_ref, sink_ref, o_ref, s_ref, p_ref, ot_ref,
                   *, tiles_per_seq):
    i = pl.program_id(0)
    pos = i % tiles_per_seq
    first = pos == 0
    last = pos == tiles_per_seq - 1
    nsub = TQ_C // WINDOW

    kk = jnp.concatenate([kp_ref[...], k_ref[...], kn_ref[...]], axis=0)
    vt = jnp.concatenate([vp_ref[...], v_ref[...], vn_ref[...]], axis=0).astype(F32).T.astype(BF16)
    qt = q_ref[...].astype(F32).T
    zeros = jnp.zeros((HEAD_C, TQ_C), F32)
    sink = sink_ref[...]
    tt = lax.broadcasted_iota(jnp.int32, (3 * WINDOW, WINDOW), 0)
    aa = lax.broadcasted_iota(jnp.int32, (3 * WINDOW, WINDOW), 1)
    rel_ok = jnp.abs(tt - WINDOW - aa) <= WINDOW

    problems = [(j, h) for j in range(nsub) for h in range(4)]
    qz = []
    for h in range(4):
        qh = qt[h * HEAD_C:(h + 1) * HEAD_C, :]
        qz.append(jnp.concatenate([qh, zeros] if h // 2 == 0 else [zeros, qh], axis=0).astype(BF16))
    for n, (j, h) in enumerate(problems):
        r0 = j * WINDOW
        s_ref[n] = jnp.dot(kk[r0:r0 + 3 * WINDOW, :], qz[h][:, r0:r0 + WINDOW], preferred_element_type=F32)
    for n, (j, h) in enumerate(problems):
        ok = rel_ok
        if j == 0:
            ok = ok & (tt >= jnp.where(first, WINDOW, 0))
        if j == nsub - 1:
            ok = ok & (tt < jnp.where(last, 2 * WINDOW, 3 * WINDOW))
        s = jnp.where(ok, s_ref[n], NEG)
        sk = sink[h:h + 1, :]
        m = jnp.maximum(jnp.max(s, axis=0, keepdims=True), sk)
        p = jnp.exp2(s - m)
        den = jnp.sum(p, axis=0, keepdims=True) + jnp.exp2(sk - m)
        p_ref[n] = (p / den).astype(BF16)
    for n, (j, h) in enumerate(problems):
        r0 = j * WINDOW
        vh = vt[(h // 2) * HEAD_C:(h // 2 + 1) * HEAD_C, r0:r0 + 3 * WINDOW]
        ot_ref[h * HEAD_C:(h + 1) * HEAD_C, r0:r0 + WINDOW] = jnp.dot(vh, p_ref[n], preferred_element_type=F32)
    o_ref[...] = ot_ref[...].T.astype(BF16)


def _attn_c(qc, kc, vc, sink, seq):
    t = qc.shape[0]
    tiles_per_seq = seq // TQ_C
    r = TQ_C // WINDOW
    nh = t // WINDOW
    row = lambda i: (i, 0)
    prev = lambda i: (jnp.maximum(i * r - 1, 0), 0)
    nxt = lambda i: (jnp.minimum((i + 1) * r, nh - 1), 0)
    kv_specs = [pl.BlockSpec((WINDOW, 128), prev), pl.BlockSpec((TQ_C, 128), row), pl.BlockSpec((WINDOW, 128), nxt)]
    return pl.pallas_call(
        functools.partial(_attn_c_kernel, tiles_per_seq=tiles_per_seq),
        grid=(t // TQ_C,),
        in_specs=[pl.BlockSpec((TQ_C, BRANCH), row)] + kv_specs + kv_specs + [_const_spec((8, 128))],
        out_specs=pl.BlockSpec((TQ_C, BRANCH), row),
        out_shape=jax.ShapeDtypeStruct((t, BRANCH), BF16),
        scratch_shapes=[pltpu.VMEM((4 * r, 3 * WINDOW, WINDOW), F32), pltpu.VMEM((4 * r, 3 * WINDOW, WINDOW), BF16),
                        pltpu.VMEM((BRANCH, TQ_C), F32)],
        compiler_params=_cparams(("parallel",)),
        name="attn_c",
    )(qc, kc, kc, kc, vc, vc, vc, sink)


def _attn_d_kernel(f8ok_ref, lam_ref, gs_ref, qt_ref, k_ref, k8_ref, vt_ref, o_ref, s_ref, mx_ref, m_ref, acc_ref,
                   ot_ref, *, nkt, lam_init):
    qt = qt_ref[...].astype(F32)
    m_ref[...] = jnp.full(m_ref.shape, NEG, F32)
    acc_ref[...] = jnp.zeros(acc_ref.shape, F32)

    def consume(vt, slot, g):
        h = g // 2
        m_old = m_ref[g]
        m_new = jnp.maximum(m_old, jnp.max(mx_ref[slot, g], axis=0, keepdims=True))
        alpha = jnp.exp2(m_old - m_new)
        p = jnp.exp2(s_ref[slot, g] - m_new[0:1])
        pv = jnp.dot(vt[h * VA_D:(h + 1) * VA_D, :], p.astype(BF16), preferred_element_type=F32)
        acc_ref[g] = acc_ref[g] * alpha[0:1] + pv
        m_ref[g] = m_new

    def run(keys_ref, score):
        def scores(kk, slot, g):
            s = score(kk, g)
            s_ref[slot, g] = s
            mx_ref[slot, g] = jnp.max(s.reshape(TK_D // 8, 8, TQ_D), axis=0)

        kk0 = keys_ref[0]
        for g in range(8):
            scores(kk0, 0, g)

        def body(u, carry):
            for j in range(UNROLL_D):
                kt = UNROLL_D * u + j
                kk = keys_ref[jnp.minimum(kt + 1, nkt - 1)]
                vt = vt_ref[kt]
                for g in range(8):
                    scores(kk, (j + 1) % 2, g)
                    consume(vt, j % 2, g)
            return carry

        lax.fori_loop(0, nkt // UNROLL_D, body, 0)

    @pl.when(f8ok_ref[0] != 0)
    def _():
        q8 = []
        for g in range(8):
            qg = qt[g * HEAD_D:(g + 1) * HEAD_D, :] * F8_Q_SCALE
            hi = qg.astype(F8).astype(F32)
            lo = (qg - hi).astype(F8).astype(F32)
            q8.append(jnp.concatenate([hi, lo, hi, lo], axis=0).astype(F8))
        run(k8_ref, lambda kk, g: jnp.dot(kk[:, g * 128:(g + 1) * 128], q8[g], preferred_element_type=F32))

    @pl.when(f8ok_ref[0] == 0)
    def _():
        grp = lax.broadcasted_iota(jnp.int32, (128, TQ_D), 0) // HEAD_D
        qz = []
        for g in range(8):
            half = qt[(g // 4) * 128:(g // 4 + 1) * 128, :]
            qz.append(jnp.where(grp == (g % 4), half, 0.0).astype(BF16))
        run(k_ref, lambda kk, g: jnp.dot(kk[:, (g // 4) * 128:(g // 4 + 1) * 128], qz[g],
                                         preferred_element_type=F32))

    lf = lam_ref[...]
    lam = (jnp.exp(jnp.sum(lf[0:1] * lf[1:2], axis=-1, keepdims=True))
           - jnp.exp(jnp.sum(lf[2:3] * lf[3:4], axis=-1, keepdims=True)) + lam_init)
    gs = gs_ref[...]
    for h in range(N_HEADS_D):
        a1 = acc_ref[2 * h]
        a2 = acc_ref[2 * h + 1]
        o = a1[0:V_D] / a1[V_D:V_D + 1] - lam * (a2[0:V_D] / a2[V_D:V_D + 1])
        ms = jnp.mean(o * o, axis=0, keepdims=True)
        ot_ref[h * V_D:(h + 1) * V_D, :] = o * lax.rsqrt(ms + EPS) * gs * (1.0 - lam_init)
    o_ref[...] = ot_ref[...].T.astype(BF16)


def _attn_d(f8ok, qdt, kd, k8, vdt, lam_d, gs, lam_init, b, s):
    nkt = s // TK_D
    nqt = s // TQ_D
    assert nkt % UNROLL_D == 0 and UNROLL_D % 2 == 0
    resident = lambda shape: pl.BlockSpec((None,) + shape, lambda bi, qi: (bi, 0, 0, 0), pipeline_mode=pl.Buffered(1))
    return pl.pallas_call(
        functools.partial(_attn_d_kernel, nkt=nkt, lam_init=lam_init),
        grid=(b, nqt),
        in_specs=[
            pl.BlockSpec(memory_space=pltpu.SMEM),
            _const_spec((4, HEAD_D)),
            _const_spec((V_D, TQ_D)),
            pl.BlockSpec((256, TQ_D), lambda bi, qi: (0, bi * nqt + qi)),
            resident((nkt, TK_D, 256)),
            resident((nkt, TK_D, 1024)),
            resident((nkt, N_HEADS_D * VA_D, TK_D)),
        ],
        out_specs=pl.BlockSpec((TQ_D, 256), lambda bi, qi: (bi * nqt + qi, 0)),
        out_shape=jax.ShapeDtypeStruct((b * s, 256), BF16),
        scratch_shapes=[pltpu.VMEM((2, 8, TK_D, TQ_D), F32), pltpu.VMEM((2, 8, 8, TQ_D), F32),
                        pltpu.VMEM((8, 8, TQ_D), F32), pltpu.VMEM((8, VA_D, TQ_D), F32),
                        pltpu.VMEM((N_HEADS_D * V_D, TQ_D), F32)],
        compiler_params=_cparams(("parallel", "arbitrary")),
        name="attn_d",
    )(f8ok, lam_d, gs, qdt, kd.reshape(b, nkt, TK_D, 256), k8.reshape(b, nkt, TK_D, 1024),
      vdt.reshape(b, nkt, N_HEADS_D * VA_D, TK_D))


def _merge_kernel(x_ref, mod_ref, gn_ref, ya_ref, yb_ref, yc_ref, yd_ref, wm_ref, wb_ref, wo_ref, o_ref):
    mod = mod_ref[...]
    gn = gn_ref[...]
    half = TM // 2

    def gated(r0):
        hb = _modulated_norm(x_ref[r0:r0 + half, :], gn, mod[0:1], mod[1:2]).astype(BF16)
        merged = None
        for i, y_ref in enumerate((ya_ref, yb_ref, yc_ref, yd_ref)):
            gate = jax.nn.sigmoid(jnp.dot(hb, wm_ref[i], preferred_element_type=F32))
            term = gate * jnp.dot(y_ref[r0:r0 + half, :], wb_ref[i], preferred_element_type=F32)
            merged = term if merged is None else merged + term
        return merged.astype(BF16)

    halves = [gated(0), gated(half)]
    for n, r0 in enumerate((0, half)):
        out = jnp.dot(halves[n], wo_ref[...], preferred_element_type=F32)
        o_ref[r0:r0 + half, :] = x_ref[r0:r0 + half, :] + mod[2:3] * out


def _merge(x2, mod, gn, ya, yb, yc, yd, wm, wb, wo, seq):
    t = x2.shape[0]
    tiles_per_seq = seq // TM
    row = lambda i: (i, 0)
    return pl.pallas_call(
        _merge_kernel,
        grid=(t // TM,),
        in_specs=[
            pl.BlockSpec((TM, D_MODEL), row),
            pl.BlockSpec((None, 8, D_MODEL), lambda i: (i // tiles_per_seq, 0, 0)),
            _const_spec((1, D_MODEL)),
        ] + [pl.BlockSpec((TM, BRANCH), row)] * 4 + [
            _const_spec((4, D_MODEL, D_MODEL)),
            _const_spec((4, BRANCH, D_MODEL)),
            _const_spec((D_MODEL, D_MODEL)),
        ],
        out_specs=pl.BlockSpec((TM, D_MODEL), row),
        out_shape=jax.ShapeDtypeStruct((t, D_MODEL), F32),
        compiler_params=_cparams(("parallel",)),
        name="merge",
    )(x2, mod, gn, ya, yb, yc, yd, wm, wb, wo)


def _ffn_kernel(xp_ref, x_ref, xn_ref, mod_ref, gn_ref, wup_ref, cw_ref, wdn_ref, o_ref, hbuf, ug, uv, act, acc,
                *, tiles_per_seq):
    i = pl.program_id(0)
    pos = i % tiles_per_seq
    keep_prev = (pos != 0).astype(F32)
    keep_next = (pos != tiles_per_seq - 1).astype(F32)
    mod = mod_ref[...]
    gn = gn_ref[...]
    x = x_ref[...]
    norm = lambda v: _modulated_norm(v, gn, mod[3:4], mod[4:5])
    hbuf[0:HALO, :] = (norm(xp_ref[...]) * keep_prev).astype(BF16)
    hbuf[HALO:HALO + TM, :] = norm(x).astype(BF16)
    hbuf[HALO + TM:, :] = (norm(xn_ref[...]) * keep_next).astype(BF16)
    hb = hbuf[...]

    def conv(buf, slot, r0, width, col):
        w = cw_ref[:, col:col + width]
        return _dwconv_rows(lambda a, b: buf[slot, a:b, 0:width], [w[k:k + 1] for k in range(3)], (-1, 0, 1),
                            HALO + r0, FF_ROWS) + w[3:4]

    starts = [sum(FF_CHUNKS[:c]) for c in range(len(FF_CHUNKS))]

    def up(c):
        c0, width, slot = starts[c], FF_CHUNKS[c], c % 2
        ug[slot, :, 0:width] = jnp.dot(hb, wup_ref[:, c0:c0 + width], preferred_element_type=F32)
        uv[slot, :, 0:width] = jnp.dot(hb, wup_ref[:, D_FF + c0:D_FF + c0 + width], preferred_element_type=F32)

    up(0)
    for c, (c0, width) in enumerate(zip(starts, FF_CHUNKS)):
        if c + 1 < len(FF_CHUNKS):
            up(c + 1)
        for r0 in range(0, TM, FF_ROWS):
            g = conv(ug, c % 2, r0, width, c0)
            v = conv(uv, c % 2, r0, width, D_FF + c0)
            act[c % 2, r0:r0 + FF_ROWS, 0:width] = (g * jax.nn.sigmoid(g) * v).astype(BF16)
        part = jnp.dot(act[c % 2, :, 0:width], wdn_ref[c0:c0 + width, :], preferred_element_type=F32)
        if c == 0:
            acc[...] = part
        else:
            acc[...] += part
    o_ref[...] = x + mod[5:6] * acc[...]


def _ffn(x2, mod, gn, wup, cw, wdn, seq):
    t = x2.shape[0]
    tiles_per_seq = seq // TM
    r = TM // HALO
    nh = t // HALO
    row = lambda i: (i, 0)
    prev = lambda i: (jnp.maximum(i * r - 1, 0), 0)
    nxt = lambda i: (jnp.minimum((i + 1) * r, nh - 1), 0)
    chunk = max(FF_CHUNKS)
    return pl.pallas_call(
        functools.partial(_ffn_kernel, tiles_per_seq=tiles_per_seq),
        grid=(t // TM,),
        in_specs=[
            pl.BlockSpec((HALO, D_MODEL), prev), pl.BlockSpec((TM, D_MODEL), row), pl.BlockSpec((HALO, D_MODEL), nxt),
            pl.BlockSpec((None, 8, D_MODEL), lambda i: (i // tiles_per_seq, 0, 0)),
            _const_spec((1, D_MODEL)),
            _const_spec((D_MODEL, 2 * D_FF)),
            _const_spec((8, 2 * D_FF)),
            _const_spec((D_FF, D_MODEL)),
        ],
        out_specs=pl.BlockSpec((TM, D_MODEL), row),
        out_shape=jax.ShapeDtypeStruct((t, D_MODEL), F32),
        scratch_shapes=[pltpu.VMEM((TM + 2 * HALO, D_MODEL), BF16),
                        pltpu.VMEM((2, TM + 2 * HALO, chunk), F32), pltpu.VMEM((2, TM + 2 * HALO, chunk), F32),
                        pltpu.VMEM((2, TM, chunk), BF16), pltpu.VMEM((TM, D_MODEL), F32)],
        compiler_params=_cparams(("parallel",)),
        name="ffn",
    )(x2, x2, x2, mod, gn, wup, cw, wdn)


def _swap_halves(a, head_dim):
    shp = a.shape
    a = a.reshape(shp[:-1] + (shp[-1] // head_dim, 2, head_dim // 2))
    return a[..., ::-1, :].reshape(shp)


def _rope_table(seq):
    pos = jnp.arange(seq, dtype=F32)[:, None]

    def one(dim):
        inv = 1.0 / (ROPE_THETA ** (jnp.arange(0, dim, 2, dtype=F32) / dim))
        ang = pos * inv[None, :]
        c, s = jnp.cos(ang), jnp.sin(ang)
        rep = 128 // dim
        return jnp.tile(jnp.concatenate([c, c], 1), (1, rep)), jnp.tile(jnp.concatenate([-s, s], 1), (1, rep))

    cc, sc = one(HEAD_C)
    cd, sd = one(HEAD_D)
    return jnp.concatenate([cc, sc, cd, sd], axis=1)


def _layer_params(l, w_in, conv_a, conv_b, conv_b_bias, ln_b_gain, ln_b_bias, g_q_c, g_k_c, sink_c, g_q_d, g_k_d,
                  g_subln_d, w_merge, w_branch, w_out, w_up, conv_ffn, conv_ffn_bias, w_down):
    def tiled(g, scale):
        g = g.astype(F32) * scale
        rep = 256 // g.shape[0]
        return jnp.tile(g, rep), jnp.tile(_swap_halves(g, g.shape[0]), rep)

    rows = (tiled(g_q_c[l], HEAD_C ** -0.5 * LOG2E) + tiled(g_k_c[l], 1.0)
            + tiled(g_q_d[l], HEAD_D ** -0.5 * LOG2E) + tiled(g_k_d[l], 1.0))
    gains = jnp.stack(rows, axis=0)
    pad_rows = lambda a, n: jnp.pad(a.astype(F32), ((0, n - a.shape[0]), (0, 0)))
    return dict(
        w_in=w_in[l].astype(BF16), gains=gains,
        f8ok=((jnp.max(jnp.abs(g_q_d[l])) <= F8_MAX_GAIN) & (jnp.max(jnp.abs(g_k_d[l])) <= F8_MAX_GAIN)
              ).astype(jnp.int32).reshape(1),
        conv_a=pad_rows(conv_a[l], 8), conv_b=pad_rows(conv_b[l], 32),
        bvec=pad_rows(jnp.stack([conv_b_bias[l], ln_b_gain[l], ln_b_bias[l]], 0), 8),
        sink=jnp.broadcast_to(pad_rows(sink_c[l][:, None] * LOG2E, 8), (8, 128)),
        gs=jnp.broadcast_to(g_subln_d[l].astype(F32)[:, None], (V_D, TQ_D)),
        wm=w_merge[l].astype(BF16), wb=w_branch[l].astype(BF16), wo=w_out[l].astype(BF16),
        wup=w_up[l].astype(BF16), wdn=w_down[l].astype(BF16),
        cw=pad_rows(jnp.concatenate([conv_ffn[l], conv_ffn_bias[l][None, :]], 0), 8),
    )


def _segment_ones():
    idx = jnp.arange(256)
    seg = lambda n: (idx[:, None] // n == idx[None, :] // n).astype(BF16)
    return jnp.stack([seg(HEAD_C), seg(HEAD_D)], 0)


def _replicate4():
    col = jnp.arange(1024)
    src = (col // 128) * HEAD_D + col % HEAD_D
    return (jnp.arange(256)[:, None] == src[None, :]).astype(BF16)


def _trunk(x, mods, layers, seg, rep, g_norm_mix, g_norm_ffn, lam_d):
    b, s, d = x.shape
    x2 = x.reshape(b * s, d)
    rope = _rope_table(s)
    for l in range(DEPTH):
        p = layers[l]
        gn1 = g_norm_mix[l].astype(F32)[None, :]
        gn2 = g_norm_ffn[l].astype(F32)[None, :]
        za, zb, qc, kc, vc, qdt, kd, vdt, k8 = _proj(x2, mods[l], gn1, p["w_in"], rope, p["gains"], seg, rep, s)
        ya, yb = _mix_ab(za, zb, p["conv_a"], p["conv_b"], p["bvec"], s)
        yc = _attn_c(qc, kc, vc, p["sink"], s)
        lam_init = 0.8 - 0.6 * math.exp(-0.3 * l)
        yd = _attn_d(p["f8ok"], qdt, kd, k8, vdt, lam_d[l].astype(F32), p["gs"], lam_init, b, s)
        x2 = _merge(x2, mods[l], gn1, ya, yb, yc, yd, p["wm"], p["wb"], p["wo"], s)
        x2 = _ffn(x2, mods[l], gn2, p["wup"], p["cw"], p["wdn"], s)
    return x2.reshape(b, s, d)


def kernel(x_prompt, x_sample, c_prompt, c_sample, w_ada, b_ada, g_norm_mix, w_in, conv_a, conv_b, conv_b_bias,
           ln_b_gain, ln_b_bias, g_q_c, g_k_c, sink_c, g_q_d, g_k_d, lam_d, g_subln_d, w_merge, w_branch, w_out,
           g_norm_ffn, w_up, conv_ffn, conv_ffn_bias, w_down):
    bp, bs = x_prompt.shape[0], x_sample.shape[0]
    assert bp + bs <= 8
    c8 = jnp.concatenate([c_prompt, c_sample, jnp.zeros((8 - bp - bs, D_MODEL), F32)], axis=0)
    mod = _ada(c8, w_ada, b_ada)
    mod = mod.reshape(DEPTH, 8, N_ADA, D_MODEL)
    mod = jnp.pad(mod, ((0, 0), (0, 0), (0, 8 - N_ADA), (0, 0)))
    layers = [_layer_params(l, w_in, conv_a, conv_b, conv_b_bias, ln_b_gain, ln_b_bias, g_q_c, g_k_c, sink_c, g_q_d,
                            g_k_d, g_subln_d, w_merge, w_branch, w_out, w_up, conv_ffn, conv_ffn_bias, w_down)
              for l in range(DEPTH)]
    seg = _segment_ones()
    rep = _replicate4()
    y_prompt = _trunk(x_prompt, [mod[l, :bp] for l in range(DEPTH)], layers, seg, rep, g_norm_mix, g_norm_ffn, lam_d)
    y_sample = _trunk(x_sample, [mod[l, bp:bp + bs] for l in range(DEPTH)], layers, seg, rep, g_norm_mix,
                      g_norm_ffn, lam_d)
    return (y_prompt, y_sample)
```
